```python
import jax, jax.numpy as jnp
from jax import lax
import numpy as np

D_MODEL = 1024
BATCH = 8
SEQ = 2048
DEPTH = 2

HEAD_DIM = 64
N_HEADS = D_MODEL // HEAD_DIM
N_A_LAYERS = max(1, DEPTH // 2)
N_B_LAYERS = DEPTH - N_A_LAYERS
DECAY_LORA = 64
ICLR_LORA = 64
N_SHIFT_MIX = 6
DIL_GROUPS = ((128, 1), (512, 4), (2048, 16))
N_GROUPS = len(DIL_GROUPS)
BAND_BLOCK = 128
ROPE_THETA = 10000.0
NORM_EPS = 1e-6
GN_EPS = 64e-5
NEG_INF = -1e30

kernel_name = "yoco_rwkv7_dilated_hybrid"


def _rms(x, g):
    xf = x.astype(jnp.float32)
    return xf * lax.rsqrt(jnp.mean(xf * xf, axis=-1, keepdims=True) + NORM_EPS) * g.astype(jnp.float32)


def _adaln(c, w, b):
    mod = jax.nn.silu(c.astype(jnp.float32)) @ w + b
    shift, scale, gate = jnp.split(mod, 3, axis=-1)
    return shift[:, None, :], scale[:, None, :], gate[:, None, :]


def _rope_tables(seq):
    pos = jnp.arange(seq, dtype=jnp.float32)
    inv = ROPE_THETA ** (-jnp.arange(0, HEAD_DIM, 2, dtype=jnp.float32) / HEAD_DIM)
    ang = pos[:, None] * inv[None, :]
    return jnp.cos(ang), jnp.sin(ang)


def _rope(x, cos, sin):
    c, s = cos[None, :, None, :], sin[None, :, None, :]
    x1, x2 = x[..., : HEAD_DIM // 2], x[..., HEAD_DIM // 2 :]
    return jnp.concatenate([x1 * c - x2 * s, x2 * c + x1 * s], axis=-1)


def _wkv7_scan(r, w, k, v, a, b):
    B, S, H, N = r.shape
    seq_major = lambda t: jnp.moveaxis(t, 1, 0)

    def step(state, inp):
        r_t, w_t, k_t, v_t, a_t, b_t = inp
        sa = jnp.einsum('bhvk,bhk->bhv', state, a_t)
        state = (state * w_t[:, :, None, :]
                 + sa[..., None] * b_t[:, :, None, :]
                 + v_t[..., None] * k_t[:, :, None, :])
        return state, jnp.einsum('bhvk,bhk->bhv', state, r_t)

    init = jnp.zeros((B, H, N, N), jnp.float32)
    _, ys = lax.scan(step, init, tuple(seq_major(t) for t in (r, w, k, v, a, b)))
    return jnp.moveaxis(ys, 0, 1)


def _rwkv7_time_mix(h, mix_mu, w_in, w0, w1, w2, a0, a1, a2, k_k, k_a, r_k, ln_g, ln_b, w_out):
    B, S, D = h.shape
    hf = h.astype(jnp.float32)
    xx = jnp.pad(hf, ((0, 0), (1, 0), (0, 0)))[:, :-1] - hf
    xs = hf[None] + xx[None] * mix_mu.astype(jnp.float32)[:, None, None, :]
    proj = jnp.einsum('pbsd,dpe->pbse', xs[:4], w_in.reshape(D, 4, D))
    r, k, v, g = proj[0], proj[1], proj[2], proj[3]
    w_log = -jax.nn.softplus(-(w0 + jnp.tanh(xs[4] @ w1) @ w2)) - 0.5
    decay = jnp.exp(-jnp.exp(w_log))
    a = jax.nn.sigmoid(a0 + (xs[5] @ a1) @ a2)
    heads = lambda t: t.reshape(B, S, N_HEADS, HEAD_DIM)
    kk = heads(k * k_k)
    kk = kk / jnp.maximum(jnp.sqrt(jnp.sum(kk * kk, axis=-1, keepdims=True)), 1e-12)
    k = k * (1.0 + (a - 1.0) * k_a)
    r, k, v, decay, a = heads(r), heads(k), heads(v), heads(decay), heads(a)
    y = _wkv7_scan(r, decay, k, v, -kk, kk * a)
    mu = jnp.mean(y, axis=-1, keepdims=True)
    var = jnp.mean(jnp.square(y - mu), axis=-1, keepdims=True)
    y = ((y - mu) * lax.rsqrt(var + GN_EPS)).reshape(B, S, D) * ln_g + ln_b
    bonus = jnp.sum(r * k * r_k, axis=-1, keepdims=True) * v
    y = (y + bonus.reshape(B, S, D)) * jax.nn.silu(g)
    return y @ w_out


def _dilated_band_attention(q, k, v, dil, win_sub):
    B, S, H, Dh = q.shape
    L = S // dil
    nb = -(-L // BAND_BLOCK)
    Lp = nb * BAND_BLOCK

    def by_residue(t):
        return t.reshape(B, L, dil, H, Dh).transpose(0, 2, 3, 1, 4)

    qb = jnp.pad(by_residue(q), ((0, 0), (0, 0), (0, 0), (0, Lp - L), (0, 0)))
    qb = qb.reshape(B, dil, H, nb, BAND_BLOCK, Dh)

    def band(t):
        tp = jnp.pad(by_residue(t), ((0, 0), (0, 0), (0, 0), (BAND_BLOCK, Lp - L), (0, 0)))
        tp = tp.reshape(B, dil, H, nb + 1, BAND_BLOCK, Dh)
        return jnp.concatenate([tp[:, :, :, :-1], tp[:, :, :, 1:]], axis=-2)

    kb, vb = band(k), band(v)
    s = jnp.einsum('bdhnqe,bdhnke->bdhnqk', qb, kb)
    qi = jnp.arange(BAND_BLOCK)[:, None]
    kj = jnp.arange(2 * BAND_BLOCK)[None, :]
    diff = BAND_BLOCK + qi - kj
    key_pos = (jnp.arange(nb)[:, None, None] - 1) * BAND_BLOCK + kj[None]
    valid = (diff >= 0) & (diff <= win_sub) & (key_pos >= 0)
    s = jnp.where(valid, s, NEG_INF)
    m = jnp.max(s, axis=-1)
    p = jnp.exp(s - m[..., None])
    l = jnp.sum(p, axis=-1)
    o = jnp.einsum('bdhnqk,bdhnke->bdhnqe', p, vb) / l[..., None]
    o = o.reshape(B, dil, H, Lp, Dh)[:, :, :, :L].transpose(0, 3, 1, 2, 4).reshape(B, S, H, Dh)
    back = lambda t: t.reshape(B, dil, H, Lp)[..., :L].transpose(0, 3, 1, 2).reshape(B, S, H)
    return o, back(m), back(l)


def _dilated_mixer(h, k_sh, v_sh, w_in, q_norm_g, w_out, cos, sin):
    B, S, D = h.shape
    proj = h.astype(jnp.float32) @ w_in
    q = proj[..., : N_GROUPS * D].reshape(B, S, N_GROUPS * N_HEADS, HEAD_DIM)
    gate = proj[..., N_GROUPS * D :]
    q = _rope(_rms(q, q_norm_g), cos, sin) * (HEAD_DIM ** -0.5)
    q = q.reshape(B, S, N_GROUPS, N_HEADS, HEAD_DIM)
    outs, maxes, denoms = [], [], []
    for gi, (win, dil) in enumerate(DIL_GROUPS):
        o, m, l = _dilated_band_attention(q[:, :, gi], k_sh, v_sh, dil, win // dil)
        outs.append(o)
        maxes.append(m)
        denoms.append(l)
    m_all = jnp.stack(maxes)
    wgt = jnp.exp(m_all - jnp.max(m_all, axis=0, keepdims=True)) * jnp.stack(denoms)
    out = jnp.einsum('gbsh,gbshe->bshe', wgt, jnp.stack(outs)) / jnp.sum(wgt, axis=0)[..., None]
    y = out.reshape(B, S, D) * jax.nn.silu(gate)
    return y @ w_out


def setup_inputs(seed: int = 0) -> dict:
    key = jax.random.key(seed)
    ks = jax.random.split(key, 32)
    D, nA, nB = D_MODEL, N_A_LAYERS, N_B_LAYERS
    f32 = jnp.float32
    nrm = lambda k, shape, s: jax.random.normal(k, shape, f32) * s
    return {
        "x": nrm(ks[0], (BATCH, SEQ, D), 1.0),
        "c": nrm(ks[1], (BATCH, D), 1.0),
        "a_ada_w": nrm(ks[2], (nA, D, 3 * D), 0.5 * D ** -0.5),
        "a_ada_b": nrm(ks[3], (nA, 3 * D), 0.02),
        "a_norm_g": 1.0 + nrm(ks[4], (nA, D), 0.02),
        "a_mix_mu": jax.random.uniform(ks[5], (nA, N_SHIFT_MIX, D), f32),
        "a_w_in": nrm(ks[6], (nA, D, 4 * D), D ** -0.5),
        "a_w0": -6.5 + 5.0 * jax.random.uniform(ks[7], (nA, D), f32),
        "a_w1": nrm(ks[8], (nA, D, DECAY_LORA), D ** -0.5),
        "a_w2": nrm(ks[9], (nA, DECAY_LORA, D), 0.5 * DECAY_LORA ** -0.5),
        "a_a0": nrm(ks[10], (nA, D), 0.1),
        "a_a1": nrm(ks[11], (nA, D, ICLR_LORA), D ** -0.5),
        "a_a2": nrm(ks[12], (nA, ICLR_LORA, D), 0.5 * ICLR_LORA ** -0.5),
        "a_k_k": 0.85 + nrm(ks[13], (nA, D), 0.02),
        "a_k_a": 1.0 + nrm(ks[14], (nA, D), 0.02),
        "a_r_k": nrm(ks[15], (nA, N_HEADS, HEAD_DIM), 0.1),
        "a_ln_g": 1.0 + nrm(ks[16], (nA, D), 0.02),
        "a_ln_b": nrm(ks[17], (nA, D), 0.02),
        "a_w_out": nrm(ks[18], (nA, D, D), D ** -0.5),
        "kv_norm_g": 1.0 + nrm(ks[19], (D,), 0.02),
        "w_kv": nrm(ks[20], (D, 2 * D), D ** -0.5),
        "k_norm_g": 1.0 + nrm(ks[21], (HEAD_DIM,), 0.02),
        "b_ada_w": nrm(ks[22], (nB, D, 3 * D), 0.5 * D ** -0.5),
        "b_ada_b": nrm(ks[23], (nB, 3 * D), 0.02),
        "b_norm_g": 1.0 + nrm(ks[24], (nB, D), 0.02),
        "b_w_in": nrm(ks[25], (nB, D, (N_GROUPS + 1) * D), D ** -0.5),
        "b_q_norm_g": 1.0 + nrm(ks[26], (nB, HEAD_DIM), 0.02),
        "b_w_out": nrm(ks[27], (nB, D, D), D ** -0.5),
    }


def reference(x, c, a_ada_w, a_ada_b, a_norm_g, a_mix_mu, a_w_in, a_w0, a_w1, a_w2, a_a0, a_a1, a_a2,
              a_k_k, a_k_a, a_r_k, a_ln_g, a_ln_b, a_w_out, kv_norm_g, w_kv, k_norm_g,
              b_ada_w, b_ada_b, b_norm_g, b_w_in, b_q_norm_g, b_w_out):
    B, S, D = x.shape
    cos, sin = _rope_tables(S)
    xr = x.astype(jnp.float32)
    k_sh = None
    v_sh = None
    for layer in range(DEPTH):
        if layer < N_A_LAYERS:
            i = layer
            shift, scale, gate = _adaln(c, a_ada_w[i], a_ada_b[i])
            h = _rms(xr, a_norm_g[i]) * (1.0 + scale) + shift
            xr = xr + gate * _rwkv7_time_mix(
                h, a_mix_mu[i], a_w_in[i], a_w0[i], a_w1[i], a_w2[i], a_a0[i], a_a1[i], a_a2[i],
                a_k_k[i], a_k_a[i], a_r_k[i], a_ln_g[i], a_ln_b[i], a_w_out[i])
            if layer == N_A_LAYERS - 1:
                kv = _rms(xr, kv_norm_g) @ w_kv
                k_sh = kv[..., :D].reshape(B, S, N_HEADS, HEAD_DIM)
                v_sh = kv[..., D:].reshape(B, S, N_HEADS, HEAD_DIM)
                k_sh = _rope(_rms(k_sh, k_norm_g), cos, sin)
        else:
            j = layer - N_A_LAYERS
            shift, scale, gate = _adaln(c, b_ada_w[j], b_ada_b[j])
            h = _rms(xr, b_norm_g[j]) * (1.0 + scale) + shift
            xr = xr + gate * _dilated_mixer(h, k_sh, v_sh, b_w_in[j], b_q_norm_g[j], b_w_out[j], cos, sin)
    return xr.astype(x.dtype)
```

```python
import functools

import jax
import jax.numpy as jnp
from jax import lax
from jax.experimental import pallas as pl
from jax.experimental.pallas import tpu as pltpu

F32 = jnp.float32
BF16 = jnp.bfloat16

HEAD_DIM = 64
LANES = 128
HEADS_PER_GROUP = LANES // HEAD_DIM
DIL_GROUPS = ((128, 1), (512, 4), (2048, 16))
BAND_BLOCK = 128
ROPE_THETA = 10000.0
NORM_EPS = 1e-6
GN_EPS = 64e-5
NEG_INF = -1e30
CHUNK = 64
SEQ_TILE = 256
VMEM_LIMIT = 56 * 1024 * 1024


def _mm(a, b):
    return jnp.dot(a.astype(BF16), b.astype(BF16), preferred_element_type=F32)


def _mm_f32(a, b):
    return jnp.dot(a, b, preferred_element_type=F32, precision=lax.Precision.HIGHEST)


def _mm_nt(a, b):
    return lax.dot_general(a.astype(BF16), b.astype(BF16), (((1,), (1,)), ((), ())),
                           preferred_element_type=F32)


def _mm_tn(a, b):
    return lax.dot_general(a.astype(BF16), b.astype(BF16), (((0,), (0,)), ((), ())),
                           preferred_element_type=F32)


def _split_dot(x, e):
    hi = x.astype(BF16)
    lo = (x - hi.astype(F32)).astype(BF16)
    return (jnp.dot(hi, e, preferred_element_type=F32) + jnp.dot(lo, e, preferred_element_type=F32))


def _head_sum(x, e):
    parts = [_split_dot(x[:, g * LANES:(g + 1) * LANES], e) for g in range(x.shape[1] // LANES)]
    return parts[0] if len(parts) == 1 else jnp.concatenate(parts, axis=1)


def _rot_half(x):
    lane = lax.broadcasted_iota(jnp.int32, (1, LANES), 1)
    first = (lane % HEAD_DIM) < (HEAD_DIM // 2)
    parts = []
    for g in range(x.shape[1] // LANES):
        xg = x[:, g * LANES:(g + 1) * LANES]
        up = pltpu.roll(xg, LANES - HEAD_DIM // 2, axis=1)
        dn = pltpu.roll(xg, HEAD_DIM // 2, axis=1)
        parts.append(jnp.where(first, up, dn))
    return parts[0] if len(parts) == 1 else jnp.concatenate(parts, axis=1)


def _tile_lanes(t, n):
    return t if n == 1 else jnp.concatenate([t] * n, axis=1)


def _rms_rows(x):
    return x * lax.rsqrt(jnp.mean(x * x, axis=-1, keepdims=True) + NORM_EPS)


def _sigmoid(x):
    return 1.0 / (1.0 + jnp.exp(-x))


def _head_ones():
    r = lax.broadcasted_iota(jnp.int32, (LANES, LANES), 0) // HEAD_DIM
    c = lax.broadcasted_iota(jnp.int32, (LANES, LANES), 1) // HEAD_DIM
    return (r == c).astype(BF16)


def _adaln_kernel(c_ref, wa_ref, ba_ref, wb_ref, bb_ref, oa_ref, ob_ref):
    c = c_ref[...]
    sc = c * _sigmoid(c)
    oa_ref[...] = _mm_f32(sc, wa_ref[...]) + ba_ref[...]
    ob_ref[...] = _mm_f32(sc, wb_ref[...]) + bb_ref[...]


def _adaln(c, wa, ba, wb, bb):
    B, D = c.shape
    n3 = wa.shape[-1]
    tn = 512
    wspec = pl.BlockSpec((None, D, tn), lambda j: (0, 0, j))
    bspec = pl.BlockSpec((1, tn), lambda j: (0, j))
    ospec = pl.BlockSpec((B, tn), lambda j: (0, j))
    return pl.pallas_call(
        _adaln_kernel,
        grid=(n3 // tn,),
        in_specs=[pl.BlockSpec((B, D), lambda j: (0, 0)), wspec, bspec, wspec, bspec],
        out_specs=[ospec, ospec],
        out_shape=[jax.ShapeDtypeStruct((B, n3), F32)] * 2,
        name="adaln",
    )(c, wa, ba, wb, bb)


def _a_pre_kernel(x_ref, xp_ref, mod_ref, ng_ref, mu_ref, win_ref, w0_ref, w1_ref, w2_ref,
                  a0_ref, a1_ref, a2_ref, kk_ref, ka_ref,
                  r_out, lw_out, k_out, v_out, an_out, bn_out, sg_out):
    D = x_ref.shape[-1]
    i = pl.program_id(1)
    mod = mod_ref[...]
    shift, scale = mod[:, :D], mod[:, D:2 * D]
    gain = ng_ref[...] * (1.0 + scale)

    h = _rms_rows(x_ref[...]) * gain + shift
    hp = _rms_rows(xp_ref[...]) * gain + shift
    hp = jnp.where(i == 0, 0.0, hp[7:8, :])
    row = lax.broadcasted_iota(jnp.int32, h.shape, 0)
    hs = jnp.where(row == 0, hp, pltpu.roll(h, 1, axis=0))
    xx = hs - h
    mu = mu_ref[...]
    xs = [(h + xx * mu[p:p + 1, :]).astype(BF16) for p in range(6)]

    r = jnp.dot(xs[0], win_ref[:, 0 * D:1 * D], preferred_element_type=F32)
    k = jnp.dot(xs[1], win_ref[:, 1 * D:2 * D], preferred_element_type=F32)
    v = jnp.dot(xs[2], win_ref[:, 2 * D:3 * D], preferred_element_type=F32)
    g = jnp.dot(xs[3], win_ref[:, 3 * D:4 * D], preferred_element_type=F32)
    wl = _mm(jnp.tanh(_mm(xs[4], w1_ref[...])), w2_ref[...])
    al = _mm(_mm(xs[5], a1_ref[...]), a2_ref[...])

    z = -(w0_ref[...] + wl)
    softplus = jnp.maximum(z, 0.0) + jnp.log(1.0 + jnp.exp(-jnp.abs(z)))
    w_log = -softplus - 0.5
    a = _sigmoid(a0_ref[...] + al)

    kk = k * kk_ref[...]
    ss = _head_sum(kk * kk, _head_ones())
    kk = kk / jnp.maximum(jnp.sqrt(ss), 1e-12)

    r_out[...] = r
    lw_out[...] = -jnp.exp(w_log)
    k_out[...] = k * (1.0 + (a - 1.0) * ka_ref[...])
    v_out[...] = v
    an_out[...] = -kk
    bn_out[...] = kk * a
    sg_out[...] = g * _sigmoid(g)


def _a_pre(x, mod3, ng, mu, w_in, w0, w1, w2, a0, a1, a2, k_k, k_a):
    B, S, D = x.shape
    ts = SEQ_TILE
    tile = pl.BlockSpec((None, ts, D), lambda b, i: (b, i, 0))
    prev = pl.BlockSpec((None, 8, D), lambda b, i: (b, jnp.maximum(i * (ts // 8) - 1, 0), 0))
    vec = pl.BlockSpec((1, D), lambda b, i: (0, 0))

    def full(a):
        return pl.BlockSpec(a.shape, lambda b, i: (0,) * a.ndim)

    return pl.pallas_call(
        _a_pre_kernel,
        grid=(B, S // ts),
        in_specs=[tile, prev, pl.BlockSpec((None, 1, 3 * D), lambda b, i: (b, 0, 0)), vec, full(mu),
                  full(w_in), vec, full(w1), full(w2), vec, full(a1), full(a2), vec, vec],
        out_specs=[tile] * 7,
        out_shape=[jax.ShapeDtypeStruct((B, S, D), F32)] * 7,
        compiler_params=pltpu.CompilerParams(vmem_limit_bytes=VMEM_LIMIT),
        name="a_pre",
    )(x, x, mod3, ng, mu, w_in, w0, w1, w2, a0, a1, a2, k_k, k_a)


def _wkv_kernel(r_ref, lw_ref, k_ref, v_ref, a_ref, b_ref, y_ref, rh_scr, p_scr, q_scr, st_scr):
    C = CHUNK
    n_chunks = r_ref.shape[0] // C
    W2 = 2 * C

    lane = lax.broadcasted_iota(jnp.int32, (1, LANES), 1)
    head0 = lane < HEAD_DIM
    ri = lax.broadcasted_iota(jnp.int32, (W2, W2), 0)
    ci = lax.broadcasted_iota(jnp.int32, (W2, W2), 1)
    same = (ri // C) == (ci // C)
    strict = same & (ci < ri)
    incl = same & (ci <= ri)
    eye = (ri == ci).astype(F32)
    tri = (lax.broadcasted_iota(jnp.int32, (C, C), 1)
           <= lax.broadcasted_iota(jnp.int32, (C, C), 0)).astype(F32)

    def stack(x):
        return jnp.concatenate([jnp.where(head0, x, 0.0), jnp.where(head0, 0.0, x)], axis=0)

    def unstack(x):
        return x[:C, :] + x[C:, :]

    def chunk_terms(c, carry):
        rows = pl.ds(pl.multiple_of(c * C, C), C)
        r, lw, k, v, a, b = (ref[rows, :] for ref in (r_ref, lw_ref, k_ref, v_ref, a_ref, b_ref))
        cum = _mm_f32(tri, lw)
        e_cum = jnp.exp(cum)
        e_inv = jnp.exp(-cum)
        e_end = jnp.exp(cum[C - 1:C, :] - cum)
        rt = r * e_cum
        at = a * jnp.exp(cum - lw)
        kt = k * e_inv
        bt = b * e_inv
        k_end = k * e_end
        b_end = b * e_end
        w_end = e_cum[C - 1:C, :]

        at_s, rt_s, v_s = stack(at), stack(rt), stack(v)
        kb = jnp.concatenate([kt, kt, bt, bt], axis=0)
        ga = _mm_nt(at_s, kb)
        gr = _mm_nt(rt_s, kb)
        a_ak = jnp.where(strict, ga[:, :W2], 0.0)
        a_ab = jnp.where(strict, ga[:, W2:], 0.0)
        a_rk = jnp.where(incl, gr[:, :W2], 0.0)
        a_rb = jnp.where(incl, gr[:, W2:], 0.0)

        t_inv = eye + a_ab
        lp = a_ab
        n = 2
        while n < C:
            lp = _mm_f32(lp, lp)
            t_inv = t_inv + _mm_f32(t_inv, lp)
            n *= 2

        au = _mm(t_inv, jnp.concatenate([at_s, _mm(a_ak, v_s)], axis=1))
        ry = _mm(a_rb, au)
        rh = unstack(rt_s + ry[:, :LANES])
        y0 = unstack(_mm(a_rk, v_s) + ry[:, LANES:])
        ah = unstack(au[:, :LANES])
        u0 = unstack(au[:, LANES:])

        p = jnp.where(same, _mm_tn(b_end, ah), 0.0) + eye * w_end
        q = jnp.where(same, _mm_tn(jnp.concatenate([k_end, b_end], axis=0),
                                   jnp.concatenate([v, u0], axis=0)), 0.0)
        rh_scr[rows, :] = rh
        y_ref[rows, :] = y0
        p_scr[c] = p
        q_scr[c] = q
        return carry

    lax.fori_loop(0, n_chunks, chunk_terms, 0)

    st_scr[...] = jnp.zeros((LANES, LANES), F32)

    def scan_chunk(c, carry):
        rows = pl.ds(pl.multiple_of(c * C, C), C)
        st = st_scr[...]
        y_ref[rows, :] = y_ref[rows, :] + _mm(rh_scr[rows, :], st)
        st_scr[...] = _mm_f32(p_scr[c], st) + q_scr[c]
        return carry

    lax.fori_loop(0, n_chunks, scan_chunk, 0)


def _wkv(r, lw, k, v, an, bn):
    B, S, D = r.shape
    blk = pl.BlockSpec((None, S, LANES), lambda b, p: (b, 0, p))
    n_chunks = S // CHUNK
    return pl.pallas_call(
        _wkv_kernel,
        grid=(B, D // LANES),
        in_specs=[blk] * 6,
        out_specs=blk,
        out_shape=jax.ShapeDtypeStruct((B, S, D), F32),
        scratch_shapes=[pltpu.VMEM((S, LANES), F32),
                        pltpu.VMEM((n_chunks, LANES, LANES), F32),
                        pltpu.VMEM((n_chunks, LANES, LANES), F32),
                        pltpu.VMEM((LANES, LANES), F32)],
        compiler_params=pltpu.CompilerParams(vmem_limit_bytes=VMEM_LIMIT),
        name="wkv",
    )(r, lw, k, v, an, bn)


def _a_post_kernel(y_ref, r_ref, k_ref, v_ref, sg_ref, x_ref, mod_ref, lng_ref, lnb_ref, rk_ref,
                   wout_ref, kvg_ref, wkv_ref, kng_ref, cos_ref, sin_ref,
                   xr_out, ksh_out, vsh_out):
    D = x_ref.shape[-1]
    e = _head_ones()
    n_lane_groups = D // LANES
    y = y_ref[...]
    mean = _head_sum(y, e) * (1.0 / HEAD_DIM)
    d = y - mean
    var = _head_sum(d * d, e) * (1.0 / HEAD_DIM)
    yn = d * lax.rsqrt(var + GN_EPS) * lng_ref[...] + lnb_ref[...]
    bonus = _head_sum(r_ref[...] * k_ref[...] * rk_ref[...], e) * v_ref[...]
    mix = _mm((yn + bonus) * sg_ref[...], wout_ref[...])
    gate = mod_ref[...][:, 2 * D:]
    xr = x_ref[...] + gate * mix
    xr_out[...] = xr

    kv = _mm(_rms_rows(xr) * kvg_ref[...], wkv_ref[...])
    ks = kv[:, :D]
    ks = ks * lax.rsqrt(_head_sum(ks * ks, e) * (1.0 / HEAD_DIM) + NORM_EPS) * kng_ref[...]
    cos = _tile_lanes(cos_ref[...], n_lane_groups)
    sin = _tile_lanes(sin_ref[...], n_lane_groups)
    ksh_out[...] = ks * cos + _rot_half(ks) * sin
    vsh_out[...] = kv[:, D:]


def _a_post(y, r, k, v, sg, x, mod3, ln_g, ln_b, r_k, w_out, kv_g, w_kv, kn_g, cos_t, sin_t):
    B, S, D = x.shape
    ts = SEQ_TILE
    tile = pl.BlockSpec((None, ts, D), lambda b, i: (b, i, 0))
    vec = pl.BlockSpec((1, D), lambda b, i: (0, 0))
    rope = pl.BlockSpec((ts, LANES), lambda b, i: (i, 0))

    def full(a):
        return pl.BlockSpec(a.shape, lambda b, i: (0,) * a.ndim)

    return pl.pallas_call(
        _a_post_kernel,
        grid=(B, S // ts),
        in_specs=[tile] * 6 + [pl.BlockSpec((None, 1, 3 * D), lambda b, i: (b, 0, 0)), vec, vec, vec,
                               full(w_out), vec, full(w_kv), vec, rope, rope],
        out_specs=[tile] * 3,
        out_shape=[jax.ShapeDtypeStruct((B, S, D), F32)] * 3,
        compiler_params=pltpu.CompilerParams(vmem_limit_bytes=VMEM_LIMIT),
        name="a_post",
    )(y, r, k, v, sg, x, mod3, ln_g, ln_b, r_k, w_out, kv_g, w_kv, kn_g, cos_t, sin_t)


def _b_pre_kernel(x_ref, mod_ref, ng_ref, win_ref, qg_ref, cos_ref, sin_ref, q_out, sg_out):
    D = x_ref.shape[-1]
    nq = q_out.shape[-1]
    e = _head_ones()
    mod = mod_ref[...]
    shift, scale = mod[:, :D], mod[:, D:2 * D]
    h = (_rms_rows(x_ref[...]) * (ng_ref[...] * (1.0 + scale)) + shift).astype(BF16)
    cos = _tile_lanes(cos_ref[...], D // LANES)
    sin = _tile_lanes(sin_ref[...], D // LANES)
    for g in range(nq // D):
        q = jnp.dot(h, win_ref[:, g * D:(g + 1) * D], preferred_element_type=F32)
        q = q * lax.rsqrt(_head_sum(q * q, e) * (1.0 / HEAD_DIM) + NORM_EPS) * qg_ref[...]
        q_out[:, g * D:(g + 1) * D] = (q * cos + _rot_half(q) * sin) * (HEAD_DIM ** -0.5)
    gate = jnp.dot(h, win_ref[:, nq:], preferred_element_type=F32)
    sg_out[...] = gate * _sigmoid(gate)


def _b_pre(xr, mod3, ng, w_in, qn_g, cos_t, sin_t):
    B, S, D = xr.shape
    nq = w_in.shape[1] - D
    ts = SEQ_TILE
    tile = pl.BlockSpec((None, ts, D), lambda b, i: (b, i, 0))
    vec = pl.BlockSpec((1, D), lambda b, i: (0, 0))
    rope = pl.BlockSpec((ts, LANES), lambda b, i: (i, 0))
    return pl.pallas_call(
        _b_pre_kernel,
        grid=(B, S // ts),
        in_specs=[tile, pl.BlockSpec((None, 1, 3 * D), lambda b, i: (b, 0, 0)), vec,
                  pl.BlockSpec(w_in.shape, lambda b, i: (0, 0)), vec, rope, rope],
        out_specs=[pl.BlockSpec((None, ts, nq), lambda b, i: (b, i, 0)), tile],
        out_shape=[jax.ShapeDtypeStruct((B, S, nq), F32), jax.ShapeDtypeStruct((B, S, D), F32)],
        compiler_params=pltpu.CompilerParams(vmem_limit_bytes=VMEM_LIMIT),
        name="b_pre",
    )(xr, mod3, ng, w_in, qn_g, cos_t, sin_t)


def _attn_kernel(q_ref, kc_ref, kp_ref, vc_ref, vp_ref, o_out, m_out, l_out, *, win_sub):
    n = pl.program_id(2)
    blk = BAND_BLOCK
    D = q_ref.shape[-1]
    lane = lax.broadcasted_iota(jnp.int32, (1, LANES), 1)
    head0 = lane < HEAD_DIM
    qi = lax.broadcasted_iota(jnp.int32, (blk, 2 * blk), 0)
    kj = lax.broadcasted_iota(jnp.int32, (blk, 2 * blk), 1)
    diff = blk + qi - kj
    valid = (diff >= 0) & (diff <= win_sub) & (((n - 1) * blk + kj) >= 0)

    m_tile = jnp.zeros((blk, LANES), F32)
    l_tile = jnp.ones((blk, LANES), F32)
    for p in range(D // LANES):
        cols = slice(p * LANES, (p + 1) * LANES)
        q = q_ref[:, cols]
        kcat = jnp.concatenate([kp_ref[:, cols], kc_ref[:, cols]], axis=0).astype(BF16)
        vcat = jnp.concatenate([vp_ref[:, cols], vc_ref[:, cols]], axis=0).astype(BF16)
        outs = []
        for j in range(HEADS_PER_GROUP):
            qm = jnp.where(head0 if j == 0 else jnp.logical_not(head0), q, 0.0)
            s = jnp.where(valid, _mm_nt(qm, kcat), NEG_INF)
            m = jnp.max(s, axis=-1, keepdims=True)
            pr = jnp.exp(s - m)
            l = jnp.sum(pr, axis=-1, keepdims=True)
            outs.append(_mm(pr, vcat) / l)
            hsel = lane == (p * HEADS_PER_GROUP + j)
            m_tile = jnp.where(hsel, m, m_tile)
            l_tile = jnp.where(hsel, l, l_tile)
        o_out[:, cols] = jnp.where(head0, outs[0], outs[1])
    m_out[...] = m_tile
    l_out[...] = l_tile


def _attn(qg, kg, vg, win_sub):
    B, dil, L, D = qg.shape
    nb = L // BAND_BLOCK
    cur = pl.BlockSpec((None, None, BAND_BLOCK, D), lambda b, r, n: (b, r, n, 0))
    prv = pl.BlockSpec((None, None, BAND_BLOCK, D), lambda b, r, n: (b, r, jnp.maximum(n - 1, 0), 0))
    stat = pl.BlockSpec((None, None, BAND_BLOCK, LANES), lambda b, r, n: (b, r, n, 0))
    return pl.pallas_call(
        functools.partial(_attn_kernel, win_sub=win_sub),
        grid=(B, dil, nb),
        in_specs=[cur, cur, prv, cur, prv],
        out_specs=[cur, stat, stat],
        out_shape=[jax.ShapeDtypeStruct((B, dil, L, D), F32),
                   jax.ShapeDtypeStruct((B, dil, L, LANES), F32),
                   jax.ShapeDtypeStruct((B, dil, L, LANES), F32)],
        compiler_params=pltpu.CompilerParams(vmem_limit_bytes=VMEM_LIMIT),
        name=f"attn_d{dil}",
    )(qg, kg, kg, vg, vg)


def _b_post_kernel(o0_ref, o1_ref, o2_ref, m0_ref, m1_ref, m2_ref, l0_ref, l1_ref, l2_ref,
                   sg_ref, x_ref, mod_ref, wout_ref, out_ref):
    D = x_ref.shape[-1]
    ms = [m0_ref[...], m1_ref[...], m2_ref[...]]
    ls = [l0_ref[...], l1_ref[...], l2_ref[...]]
    m_all = jnp.maximum(jnp.maximum(ms[0], ms[1]), ms[2])
    wg = [jnp.exp(m - m_all) * l for m, l in zip(ms, ls)]
    inv = 1.0 / (wg[0] + wg[1] + wg[2])
    hr = lax.broadcasted_iota(jnp.int32, (LANES, D), 0)
    hc = lax.broadcasted_iota(jnp.int32, (LANES, D), 1) // HEAD_DIM
    expand = (hr == hc).astype(BF16)
    acc = None
    for w, o_ref in zip(wg, (o0_ref, o1_ref, o2_ref)):
        term = _split_dot(w * inv, expand) * o_ref[...]
        acc = term if acc is None else acc + term
    gate = mod_ref[...][:, 2 * D:]
    out_ref[...] = x_ref[...] + gate * _mm(acc * sg_ref[...], wout_ref[...])


def _b_post(os_, ms_, ls_, sg, xr, mod3, w_out):
    B, S, D = xr.shape
    ts = SEQ_TILE
    tile = pl.BlockSpec((None, ts, D), lambda b, i: (b, i, 0))
    stat = pl.BlockSpec((None, ts, LANES), lambda b, i: (b, i, 0))
    return pl.pallas_call(
        _b_post_kernel,
        grid=(B, S // ts),
        in_specs=[tile] * 3 + [stat] * 6 + [tile, tile,
                                            pl.BlockSpec((None, 1, 3 * D), lambda b, i: (b, 0, 0)),
                                            pl.BlockSpec(w_out.shape, lambda b, i: (0, 0))],
        out_specs=tile,
        out_shape=jax.ShapeDtypeStruct((B, S, D), F32),
        compiler_params=pltpu.CompilerParams(vmem_limit_bytes=VMEM_LIMIT),
        name="b_post",
    )(*os_, *ms_, *ls_, sg, xr, mod3, w_out)


def _rope_tables(seq):
    pos = jnp.arange(seq, dtype=F32)
    inv = ROPE_THETA ** (-jnp.arange(0, HEAD_DIM, 2, dtype=F32) / HEAD_DIM)
    ang = pos[:, None] * inv[None, :]
    cos, sin = jnp.cos(ang), jnp.sin(ang)
    cos_t = jnp.concatenate([cos, cos] * HEADS_PER_GROUP, axis=-1)
    sin_t = jnp.concatenate([-sin, sin] * HEADS_PER_GROUP, axis=-1)
    return cos_t, sin_t


def _by_residue(t, dil):
    B, S, D = t.shape
    return t.reshape(B, S // dil, dil, D).transpose(0, 2, 1, 3)


def _from_residue(t):
    B, dil, L, D = t.shape
    return t.transpose(0, 2, 1, 3).reshape(B, L * dil, D)


def kernel(x, c, a_ada_w, a_ada_b, a_norm_g, a_mix_mu, a_w_in, a_w0, a_w1, a_w2, a_a0, a_a1, a_a2,
           a_k_k, a_k_a, a_r_k, a_ln_g, a_ln_b, a_w_out, kv_norm_g, w_kv, k_norm_g,
           b_ada_w, b_ada_b, b_norm_g, b_w_in, b_q_norm_g, b_w_out):
    B, S, D = x.shape
    assert a_ada_w.shape[0] == 1 and b_ada_w.shape[0] == 1
    assert D % LANES == 0 and S % (BAND_BLOCK * DIL_GROUPS[-1][1]) == 0 and S % SEQ_TILE == 0
    n_groups = len(DIL_GROUPS)
    row = lambda t: t.reshape(1, -1)
    per_head = lambda t: jnp.tile(t.reshape(1, HEAD_DIM), (1, D // HEAD_DIM))

    mod_a, mod_b = _adaln(c, a_ada_w, a_ada_b, b_ada_w, b_ada_b)
    mod_a = mod_a.reshape(B, 1, 3 * D)
    mod_b = mod_b.reshape(B, 1, 3 * D)
    cos_t, sin_t = _rope_tables(S)

    r, lw, k, v, an, bn, sg_a = _a_pre(
        x, mod_a, a_norm_g, a_mix_mu[0], a_w_in[0].astype(BF16), a_w0, a_w1[0].astype(BF16),
        a_w2[0].astype(BF16), a_a0, a_a1[0].astype(BF16), a_a2[0].astype(BF16), a_k_k, a_k_a)
    y = _wkv(r, lw, k, v, an, bn)
    xr, k_sh, v_sh = _a_post(
        y, r, k, v, sg_a, x, mod_a, a_ln_g, a_ln_b, row(a_r_k[0]), a_w_out[0].astype(BF16),
        row(kv_norm_g), w_kv.astype(BF16), per_head(k_norm_g), cos_t, sin_t)

    q, sg_b = _b_pre(xr, mod_b, b_norm_g, b_w_in[0].astype(BF16), per_head(b_q_norm_g[0]), cos_t, sin_t)

    os_, ms_, ls_ = [], [], []
    for gi, (win, dil) in enumerate(DIL_GROUPS):
        qg = _by_residue(q[:, :, gi * D:(gi + 1) * D], dil)
        o, m, l = _attn(qg, _by_residue(k_sh, dil), _by_residue(v_sh, dil), win // dil)
        os_.append(_from_residue(o))
        ms_.append(_from_residue(m))
        ls_.append(_from_residue(l))
    assert len(os_) == n_groups
    return _b_post(os_, ms_, ls_, sg_b, xr, mod_b, b_w_out[0].astype(BF16))
```

```python
import functools

import jax
import jax.numpy as jnp
from jax import lax
from jax.experimental import pallas as pl
from jax.experimental.pallas import tpu as pltpu

F32 = jnp.float32
BF16 = jnp.bfloat16

HEAD_DIM = 64
LANES = 128
HEADS_PER_GROUP = LANES // HEAD_DIM
DIL_GROUPS = ((128, 1), (512, 4), (2048, 16))
BAND_BLOCK = 128
ROPE_THETA = 10000.0
NORM_EPS = 1e-6
GN_EPS = 64e-5
NEG_INF = -1e30
CHUNK = 64
SEQ_TILE = 256
VMEM_LIMIT = 56 * 1024 * 1024


def _mm(a, b):
    return jnp.dot(a.astype(BF16), b.astype(BF16), preferred_element_type=F32)


def _mm_f32(a, b):
    return jnp.dot(a, b, preferred_element_type=F32, precision=lax.Precision.HIGHEST)


def _mm_nt(a, b):
    return lax.dot_general(a.astype(BF16), b.astype(BF16), (((1,), (1,)), ((), ())),
                           preferred_element_type=F32)


def _mm_tn(a, b):
    return lax.dot_general(a.astype(BF16), b.astype(BF16), (((0,), (0,)), ((), ())),
                           preferred_element_type=F32)


def _split_dot(x, e):
    hi = x.astype(BF16)
    lo = (x - hi.astype(F32)).astype(BF16)
    return (jnp.dot(hi, e, preferred_element_type=F32) + jnp.dot(lo, e, preferred_element_type=F32))


def _split_dot_lhs(e, x):
    hi = x.astype(BF16)
    lo = (x - hi.astype(F32)).astype(BF16)
    return (jnp.dot(e, hi, preferred_element_type=F32) + jnp.dot(e, lo, preferred_element_type=F32))


def _mm_3pass(a, b):
    a_hi = a.astype(BF16)
    a_lo = (a - a_hi.astype(F32)).astype(BF16)
    b_hi = b.astype(BF16)
    b_lo = (b - b_hi.astype(F32)).astype(BF16)
    dot = functools.partial(jnp.dot, preferred_element_type=F32)
    return dot(a_hi, b_hi) + (dot(a_hi, b_lo) + dot(a_lo, b_hi))


def _head_sum(x, e):
    parts = [_split_dot(x[:, g * LANES:(g + 1) * LANES], e) for g in range(x.shape[1] // LANES)]
    return parts[0] if len(parts) == 1 else jnp.concatenate(parts, axis=1)


def _rot_half(x):
    lane = lax.broadcasted_iota(jnp.int32, (1, LANES), 1)
    first = (lane % HEAD_DIM) < (HEAD_DIM // 2)
    parts = []
    for g in range(x.shape[1] // LANES):
        xg = x[:, g * LANES:(g + 1) * LANES]
        up = pltpu.roll(xg, LANES - HEAD_DIM // 2, axis=1)
        dn = pltpu.roll(xg, HEAD_DIM // 2, axis=1)
        parts.append(jnp.where(first, up, dn))
    return parts[0] if len(parts) == 1 else jnp.concatenate(parts, axis=1)


def _tile_lanes(t, n):
    return t if n == 1 else jnp.concatenate([t] * n, axis=1)


def _rms_rows(x):
    return x * lax.rsqrt(jnp.mean(x * x, axis=-1, keepdims=True) + NORM_EPS)


def _sigmoid(x):
    return 1.0 / (1.0 + jnp.exp(-x))


def _head_ones():
    r = lax.broadcasted_iota(jnp.int32, (LANES, LANES), 0) // HEAD_DIM
    c = lax.broadcasted_iota(jnp.int32, (LANES, LANES), 1) // HEAD_DIM
    return (r == c).astype(BF16)


def _adaln_kernel(c_ref, wa_ref, ba_ref, wb_ref, bb_ref, oa_ref, ob_ref):
    c = c_ref[...]
    sc = c * _sigmoid(c)
    oa_ref[...] = _mm_f32(sc, wa_ref[...]) + ba_ref[...]
    ob_ref[...] = _mm_f32(sc, wb_ref[...]) + bb_ref[...]


def _adaln(c, wa, ba, wb, bb):
    B, D = c.shape
    n3 = wa.shape[-1]
    tn = 512
    wspec = pl.BlockSpec((None, D, tn), lambda j: (0, 0, j))
    bspec = pl.BlockSpec((1, tn), lambda j: (0, j))
    ospec = pl.BlockSpec((B, tn), lambda j: (0, j))
    return pl.pallas_call(
        _adaln_kernel,
        grid=(n3 // tn,),
        in_specs=[pl.BlockSpec((B, D), lambda j: (0, 0)), wspec, bspec, wspec, bspec],
        out_specs=[ospec, ospec],
        out_shape=[jax.ShapeDtypeStruct((B, n3), F32)] * 2,
        name="adaln",
    )(c, wa, ba, wb, bb)


def _a_pre_kernel(x_ref, xp_ref, mod_ref, ng_ref, mu_ref, win_ref, w0_ref, w1_ref, w2_ref,
                  a0_ref, a1_ref, a2_ref, kk_ref, ka_ref,
                  r_out, lw_out, k_out, v_out, an_out, bn_out, sg_out):
    D = x_ref.shape[-1]
    i = pl.program_id(1)
    mod = mod_ref[...]
    shift, scale = mod[:, :D], mod[:, D:2 * D]
    gain = ng_ref[...] * (1.0 + scale)

    h = _rms_rows(x_ref[...]) * gain + shift
    hp = _rms_rows(xp_ref[...]) * gain + shift
    hp = jnp.where(i == 0, 0.0, hp[7:8, :])
    row = lax.broadcasted_iota(jnp.int32, h.shape, 0)
    hs = jnp.where(row == 0, hp, pltpu.roll(h, 1, axis=0))
    xx = hs - h
    mu = mu_ref[...]
    xs = [(h + xx * mu[p:p + 1, :]).astype(BF16) for p in range(6)]

    r = jnp.dot(xs[0], win_ref[:, 0 * D:1 * D], preferred_element_type=F32)
    k = jnp.dot(xs[1], win_ref[:, 1 * D:2 * D], preferred_element_type=F32)
    v = jnp.dot(xs[2], win_ref[:, 2 * D:3 * D], preferred_element_type=F32)
    g = jnp.dot(xs[3], win_ref[:, 3 * D:4 * D], preferred_element_type=F32)
    wl = _mm(jnp.tanh(_mm(xs[4], w1_ref[...])), w2_ref[...])
    al = _mm(_mm(xs[5], a1_ref[...]), a2_ref[...])

    z = -(w0_ref[...] + wl)
    softplus = jnp.maximum(z, 0.0) + jnp.log(1.0 + jnp.exp(-jnp.abs(z)))
    w_log = -softplus - 0.5
    a = _sigmoid(a0_ref[...] + al)

    kk = k * kk_ref[...]
    ss = _head_sum(kk * kk, _head_ones())
    kk = kk / jnp.maximum(jnp.sqrt(ss), 1e-12)

    r_out[...] = r
    lw_out[...] = -jnp.exp(w_log)
    k_out[...] = k * (1.0 + (a - 1.0) * ka_ref[...])
    v_out[...] = v
    an_out[...] = -kk
    bn_out[...] = kk * a
    sg_out[...] = g * _sigmoid(g)


def _a_pre(x, mod3, ng, mu, w_in, w0, w1, w2, a0, a1, a2, k_k, k_a):
    B, S, D = x.shape
    ts = SEQ_TILE
    tile = pl.BlockSpec((None, ts, D), lambda b, i: (b, i, 0))
    prev = pl.BlockSpec((None, 8, D), lambda b, i: (b, jnp.maximum(i * (ts // 8) - 1, 0), 0))
    vec = pl.BlockSpec((1, D), lambda b, i: (0, 0))

    def full(a):
        return pl.BlockSpec(a.shape, lambda b, i: (0,) * a.ndim)

    return pl.pallas_call(
        _a_pre_kernel,
        grid=(B, S // ts),
        in_specs=[tile, prev, pl.BlockSpec((None, 1, 3 * D), lambda b, i: (b, 0, 0)), vec, full(mu),
                  full(w_in), vec, full(w1), full(w2), vec, full(a1), full(a2), vec, vec],
        out_specs=[tile] * 7,
        out_shape=[jax.ShapeDtypeStruct((B, S, D), F32)] * 7,
        compiler_params=pltpu.CompilerParams(vmem_limit_bytes=VMEM_LIMIT),
        name="a_pre",
    )(x, x, mod3, ng, mu, w_in, w0, w1, w2, a0, a1, a2, k_k, k_a)


WKV_UNROLL = 4


def _wkv_kernel(r_ref, lw_ref, k_ref, v_ref, a_ref, b_ref, y_ref,
                lp_scr, t_scr, aak_scr, ark_scr, arb_scr, ats_scr, vs_scr, rt_scr, kend_scr, bend_scr,
                wend_scr, rh_scr, y0_scr, p_scr, q_scr, st_scr):
    C = CHUNK
    n_chunks = r_ref.shape[0] // C
    W2 = 2 * C
    U = WKV_UNROLL

    lane = lax.broadcasted_iota(jnp.int32, (1, LANES), 1)
    head0 = lane < HEAD_DIM
    ri = lax.broadcasted_iota(jnp.int32, (W2, W2), 0)
    ci = lax.broadcasted_iota(jnp.int32, (W2, W2), 1)
    same = (ri // C) == (ci // C)
    strict = same & (ci < ri)
    incl = same & (ci <= ri)
    eye = (ri == ci).astype(F32)
    tri = (lax.broadcasted_iota(jnp.int32, (C, C), 1)
           <= lax.broadcasted_iota(jnp.int32, (C, C), 0)).astype(BF16)
    dot = functools.partial(jnp.dot, preferred_element_type=F32)

    def stack(x):
        return jnp.concatenate([jnp.where(head0, x, 0.0), jnp.where(head0, 0.0, x)], axis=0)

    def unstack(x):
        return x[:C, :] + x[C:, :]

    def chunk_rows(c):
        return pl.ds(pl.multiple_of(c * C, C), C)

    def lockstep(gens):
        gens = list(gens)
        while gens:
            alive = []
            for g in gens:
                try:
                    next(g)
                    alive.append(g)
                except StopIteration:
                    pass
            gens = alive

    def for_chunks(fn):
        def body(i, carry):
            lockstep(fn(i * U + u) for u in range(U))
            return carry
        lax.fori_loop(0, n_chunks // U, body, 0)

    def gram_terms(c):
        rows = chunk_rows(c)
        r, lw, k, v, a, b = (ref[rows, :] for ref in (r_ref, lw_ref, k_ref, v_ref, a_ref, b_ref))
        cum = _split_dot_lhs(tri, lw)
        yield
        e_cum = jnp.exp(cum)
        e_inv = jnp.exp(-cum)
        e_end = jnp.exp(cum[C - 1:C, :] - cum)
        rt = r * e_cum
        at_s = stack(a * jnp.exp(cum - lw)).astype(BF16)
        rt_s = stack(rt).astype(BF16)
        kt = (k * e_inv).astype(BF16)
        bt = (b * e_inv).astype(BF16)
        kb = jnp.concatenate([kt, kt, bt, bt], axis=0)
        nt = (((1,), (1,)), ((), ()))
        ga = lax.dot_general(at_s, kb, nt, preferred_element_type=F32)
        gr = lax.dot_general(rt_s, kb, nt, preferred_element_type=F32)
        ats_scr[c] = at_s
        vs_scr[c] = stack(v).astype(BF16)
        rt_scr[rows, :] = rt
        kend_scr[rows, :] = (k * e_end).astype(BF16)
        bend_scr[rows, :] = (b * e_end).astype(BF16)
        wend_scr[c] = jnp.broadcast_to(e_cum[C - 1:C, :], (8, LANES))
        yield
        a_ab = jnp.where(strict, ga[:, W2:], 0.0)
        lp_scr[c] = a_ab.astype(BF16)
        t_scr[c] = eye + a_ab
        aak_scr[c] = jnp.where(strict, ga[:, :W2], 0.0).astype(BF16)
        ark_scr[c] = jnp.where(incl, gr[:, :W2], 0.0).astype(BF16)
        arb_scr[c] = jnp.where(incl, gr[:, W2:], 0.0).astype(BF16)

    def square_only(c):
        lp = lp_scr[c]
        lp2 = dot(lp, lp)
        yield
        lp_scr[c] = lp2.astype(BF16)

    def fold_and_square(c):
        lp = lp_scr[c]
        t = t_scr[c]
        tl = dot(t.astype(BF16), lp)
        lp2 = dot(lp, lp)
        yield
        t_scr[c] = t + tl
        lp_scr[c] = lp2.astype(BF16)

    def fold_only(c):
        t = t_scr[c]
        tl = dot(t.astype(BF16), lp_scr[c])
        yield
        t_scr[c] = t + tl

    def chunk_maps(c):
        rows = chunk_rows(c)
        vs = vs_scr[c]
        x = dot(aak_scr[c], vs)
        y0a = dot(ark_scr[c], vs)
        yield
        au = dot(t_scr[c].astype(BF16), jnp.concatenate([ats_scr[c], x.astype(BF16)], axis=1))
        yield
        ry = dot(arb_scr[c], au.astype(BF16))
        ah = unstack(au[:, :LANES]).astype(BF16)
        u0 = unstack(au[:, LANES:]).astype(BF16)
        k_end = kend_scr[rows, :]
        b_end = bend_scr[rows, :]
        tn = (((0,), (0,)), ((), ()))
        p = lax.dot_general(b_end, ah, tn, preferred_element_type=F32)
        q = lax.dot_general(jnp.concatenate([k_end, b_end], axis=0),
                            jnp.concatenate([v_ref[rows, :].astype(BF16), u0], axis=0), tn,
                            preferred_element_type=F32)
        yield
        rh_scr[rows, :] = rt_scr[rows, :] + unstack(ry[:, :LANES])
        y0_scr[rows, :] = unstack(y0a + ry[:, LANES:])
        p_scr[c] = jnp.where(same, p, 0.0) + eye * wend_scr[c][0:1, :]
        q_scr[c] = jnp.where(same, q, 0.0)

    def scan_chunks(c0):
        for u in range(U):
            c = c0 + u
            rows = chunk_rows(c)
            st = st_scr[...]
            ys = _mm(rh_scr[rows, :], st)
            st_new = _mm_3pass(p_scr[c], st)
            yield
            y_ref[rows, :] = y0_scr[rows, :] + ys
            st_scr[...] = st_new + q_scr[c]

    for_chunks(gram_terms)
    for_chunks(square_only)
    n = 4
    while n < C:
        for_chunks(fold_and_square)
        n *= 2
    for_chunks(fold_only)

    st_scr[...] = jnp.zeros((LANES, LANES), F32)
    lockstep(chunk_maps(u) for u in range(U))

    def maps_and_scan(i, carry):
        lockstep([scan_chunks((i - 1) * U)] + [chunk_maps(i * U + u) for u in range(U)])
        return carry

    lax.fori_loop(1, n_chunks // U, maps_and_scan, 0)
    lockstep([scan_chunks(n_chunks - U)])


def _wkv(r, lw, k, v, an, bn):
    B, S, D = r.shape
    blk = pl.BlockSpec((None, S, LANES), lambda b, p: (b, 0, p))
    n_chunks = S // CHUNK
    mat = lambda dt: pltpu.VMEM((n_chunks, LANES, LANES), dt)
    seq = lambda dt: pltpu.VMEM((S, LANES), dt)
    return pl.pallas_call(
        _wkv_kernel,
        grid=(B, D // LANES),
        in_specs=[blk] * 6,
        out_specs=blk,
        out_shape=jax.ShapeDtypeStruct((B, S, D), F32),
        scratch_shapes=[mat(BF16), mat(F32), mat(BF16), mat(BF16), mat(BF16), mat(BF16), mat(BF16),
                        seq(F32), seq(BF16), seq(BF16), pltpu.VMEM((n_chunks, 8, LANES), F32),
                        seq(F32), seq(F32), mat(F32), mat(F32),
                        pltpu.VMEM((LANES, LANES), F32)],
        compiler_params=pltpu.CompilerParams(vmem_limit_bytes=VMEM_LIMIT),
        name="wkv",
    )(r, lw, k, v, an, bn)


def _a_post_kernel(y_ref, r_ref, k_ref, v_ref, sg_ref, x_ref, mod_ref, lng_ref, lnb_ref, rk_ref,
                   wout_ref, kvg_ref, wkv_ref, kng_ref, cos_ref, sin_ref,
                   xr_out, ksh_out, vsh_out):
    D = x_ref.shape[-1]
    e = _head_ones()
    n_lane_groups = D // LANES
    y = y_ref[...]
    mean = _head_sum(y, e) * (1.0 / HEAD_DIM)
    d = y - mean
    var = _head_sum(d * d, e) * (1.0 / HEAD_DIM)
    yn = d * lax.rsqrt(var + GN_EPS) * lng_ref[...] + lnb_ref[...]
    bonus = _head_sum(r_ref[...] * k_ref[...] * rk_ref[...], e) * v_ref[...]
    mix = _mm((yn + bonus) * sg_ref[...], wout_ref[...])
    gate = mod_ref[...][:, 2 * D:]
    xr = x_ref[...] + gate * mix
    xr_out[...] = xr

    kv = _mm(_rms_rows(xr) * kvg_ref[...], wkv_ref[...])
    ks = kv[:, :D]
    ks = ks * lax.rsqrt(_head_sum(ks * ks, e) * (1.0 / HEAD_DIM) + NORM_EPS) * kng_ref[...]
    cos = _tile_lanes(cos_ref[...], n_lane_groups)
    sin = _tile_lanes(sin_ref[...], n_lane_groups)
    ksh_out[...] = ks * cos + _rot_half(ks) * sin
    vsh_out[...] = kv[:, D:]


def _a_post(y, r, k, v, sg, x, mod3, ln_g, ln_b, r_k, w_out, kv_g, w_kv, kn_g, cos_t, sin_t):
    B, S, D = x.shape
    ts = SEQ_TILE
    tile = pl.BlockSpec((None, ts, D), lambda b, i: (b, i, 0))
    vec = pl.BlockSpec((1, D), lambda b, i: (0, 0))
    rope = pl.BlockSpec((ts, LANES), lambda b, i: (i, 0))

    def full(a):
        return pl.BlockSpec(a.shape, lambda b, i: (0,) * a.ndim)

    return pl.pallas_call(
        _a_post_kernel,
        grid=(B, S // ts),
        in_specs=[tile] * 6 + [pl.BlockSpec((None, 1, 3 * D), lambda b, i: (b, 0, 0)), vec, vec, vec,
                               full(w_out), vec, full(w_kv), vec, rope, rope],
        out_specs=[tile] * 3,
        out_shape=[jax.ShapeDtypeStruct((B, S, D), F32)] * 3,
        compiler_params=pltpu.CompilerParams(vmem_limit_bytes=VMEM_LIMIT),
        name="a_post",
    )(y, r, k, v, sg, x, mod3, ln_g, ln_b, r_k, w_out, kv_g, w_kv, kn_g, cos_t, sin_t)


def _b_pre_kernel(x_ref, mod_ref, ng_ref, win_ref, qg_ref, cos_ref, sin_ref, q_out, sg_out):
    D = x_ref.shape[-1]
    nq = q_out.shape[-1]
    e = _head_ones()
    mod = mod_ref[...]
    shift, scale = mod[:, :D], mod[:, D:2 * D]
    h = (_rms_rows(x_ref[...]) * (ng_ref[...] * (1.0 + scale)) + shift).astype(BF16)
    cos = _tile_lanes(cos_ref[...], D // LANES)
    sin = _tile_lanes(sin_ref[...], D // LANES)
    for g in range(nq // D):
        q = jnp.dot(h, win_ref[:, g * D:(g + 1) * D], preferred_element_type=F32)
        q = q * lax.rsqrt(_head_sum(q * q, e) * (1.0 / HEAD_DIM) + NORM_EPS) * qg_ref[...]
        q_out[:, g * D:(g + 1) * D] = (q * cos + _rot_half(q) * sin) * (HEAD_DIM ** -0.5)
    gate = jnp.dot(h, win_ref[:, nq:], preferred_element_type=F32)
    sg_out[...] = gate * _sigmoid(gate)


def _b_pre(xr, mod3, ng, w_in, qn_g, cos_t, sin_t):
    B, S, D = xr.shape
    nq = w_in.shape[1] - D
    ts = SEQ_TILE
    tile = pl.BlockSpec((None, ts, D), lambda b, i: (b, i, 0))
    vec = pl.BlockSpec((1, D), lambda b, i: (0, 0))
    rope = pl.BlockSpec((ts, LANES), lambda b, i: (i, 0))
    return pl.pallas_call(
        _b_pre_kernel,
        grid=(B, S // ts),
        in_specs=[tile, pl.BlockSpec((None, 1, 3 * D), lambda b, i: (b, 0, 0)), vec,
                  pl.BlockSpec(w_in.shape, lambda b, i: (0, 0)), vec, rope, rope],
        out_specs=[pl.BlockSpec((None, ts, nq), lambda b, i: (b, i, 0)), tile],
        out_shape=[jax.ShapeDtypeStruct((B, S, nq), F32), jax.ShapeDtypeStruct((B, S, D), F32)],
        compiler_params=pltpu.CompilerParams(vmem_limit_bytes=VMEM_LIMIT),
        name="b_pre",
    )(xr, mod3, ng, w_in, qn_g, cos_t, sin_t)


def _attn_kernel(q_ref, kc_ref, kp_ref, vc_ref, vp_ref, o_out, m_out, l_out, *, win_sub):
    n = pl.program_id(2)
    blk = BAND_BLOCK
    D = q_ref.shape[-1]
    lane = lax.broadcasted_iota(jnp.int32, (1, LANES), 1)
    head0 = lane < HEAD_DIM
    qi = lax.broadcasted_iota(jnp.int32, (blk, 2 * blk), 0)
    kj = lax.broadcasted_iota(jnp.int32, (blk, 2 * blk), 1)
    diff = blk + qi - kj
    valid = (diff >= 0) & (diff <= win_sub) & (((n - 1) * blk + kj) >= 0)

    m_tile = jnp.zeros((blk, LANES), F32)
    l_tile = jnp.ones((blk, LANES), F32)
    for p in range(D // LANES):
        cols = slice(p * LANES, (p + 1) * LANES)
        q = q_ref[:, cols]
        kcat = jnp.concatenate([kp_ref[:, cols], kc_ref[:, cols]], axis=0).astype(BF16)
        vcat = jnp.concatenate([vp_ref[:, cols], vc_ref[:, cols]], axis=0).astype(BF16)
        outs = []
        for j in range(HEADS_PER_GROUP):
            qm = jnp.where(head0 if j == 0 else jnp.logical_not(head0), q, 0.0)
            s = jnp.where(valid, _mm_nt(qm, kcat), NEG_INF)
            m = jnp.max(s, axis=-1, keepdims=True)
            pr = jnp.exp(s - m)
            l = jnp.sum(pr, axis=-1, keepdims=True)
            outs.append(_mm(pr, vcat) / l)
            hsel = lane == (p * HEADS_PER_GROUP + j)
            m_tile = jnp.where(hsel, m, m_tile)
            l_tile = jnp.where(hsel, l, l_tile)
        o_out[:, cols] = jnp.where(head0, outs[0], outs[1])
    m_out[...] = m_tile
    l_out[...] = l_tile


def _attn(qg, kg, vg, win_sub):
    B, dil, L, D = qg.shape
    nb = L // BAND_BLOCK
    cur = pl.BlockSpec((None, None, BAND_BLOCK, D), lambda b, r, n: (b, r, n, 0))
    prv = pl.BlockSpec((None, None, BAND_BLOCK, D), lambda b, r, n: (b, r, jnp.maximum(n - 1, 0), 0))
    stat = pl.BlockSpec((None, None, BAND_BLOCK, LANES), lambda b, r, n: (b, r, n, 0))
    return pl.pallas_call(
        functools.partial(_attn_kernel, win_sub=win_sub),
        grid=(B, dil, nb),
        in_specs=[cur, cur, prv, cur, prv],
        out_specs=[cur, stat, stat],
        out_shape=[jax.ShapeDtypeStruct((B, dil, L, D), F32),
                   jax.ShapeDtypeStruct((B, dil, L, LANES), F32),
                   jax.ShapeDtypeStruct((B, dil, L, LANES), F32)],
        compiler_params=pltpu.CompilerParams(vmem_limit_bytes=VMEM_LIMIT),
        name=f"attn_d{dil}",
    )(qg, kg, kg, vg, vg)


def _b_post_kernel(o0_ref, o1_ref, o2_ref, m0_ref, m1_ref, m2_ref, l0_ref, l1_ref, l2_ref,
                   sg_ref, x_ref, mod_ref, wout_ref, out_ref):
    D = x_ref.shape[-1]
    ms = [m0_ref[...], m1_ref[...], m2_ref[...]]
    ls = [l0_ref[...], l1_ref[...], l2_ref[...]]
    m_all = jnp.maximum(jnp.maximum(ms[0], ms[1]), ms[2])
    wg = [jnp.exp(m - m_all) * l for m, l in zip(ms, ls)]
    inv = 1.0 / (wg[0] + wg[1] + wg[2])
    hr = lax.broadcasted_iota(jnp.int32, (LANES, D), 0)
    hc = lax.broadcasted_iota(jnp.int32, (LANES, D), 1) // HEAD_DIM
    expand = (hr == hc).astype(BF16)
    acc = None
    for w, o_ref in zip(wg, (o0_ref, o1_ref, o2_ref)):
        term = _split_dot(w * inv, expand) * o_ref[...]
        acc = term if acc is None else acc + term
    gate = mod_ref[...][:, 2 * D:]
    out_ref[...] = x_ref[...] + gate * _mm(acc * sg_ref[...], wout_ref[...])


def _b_post(os_, ms_, ls_, sg, xr, mod3, w_out):
    B, S, D = xr.shape
    ts = SEQ_TILE
    tile = pl.BlockSpec((None, ts, D), lambda b, i: (b, i, 0))
    stat = pl.BlockSpec((None, ts, LANES), lambda b, i: (b, i, 0))
    return pl.pallas_call(
        _b_post_kernel,
        grid=(B, S // ts),
        in_specs=[tile] * 3 + [stat] * 6 + [tile, tile,
                                            pl.BlockSpec((None, 1, 3 * D), lambda b, i: (b, 0, 0)),
                                            pl.BlockSpec(w_out.shape, lambda b, i: (0, 0))],
        out_specs=tile,
        out_shape=jax.ShapeDtypeStruct((B, S, D), F32),
        compiler_params=pltpu.CompilerParams(vmem_limit_bytes=VMEM_LIMIT),
        name="b_post",
    )(*os_, *ms_, *ls_, sg, xr, mod3, w_out)


def _rope_tables(seq):
    pos = jnp.arange(seq, dtype=F32)
    inv = ROPE_THETA ** (-jnp.arange(0, HEAD_DIM, 2, dtype=F32) / HEAD_DIM)
    ang = pos[:, None] * inv[None, :]
    cos, sin = jnp.cos(ang), jnp.sin(ang)
    cos_t = jnp.concatenate([cos, cos] * HEADS_PER_GROUP, axis=-1)
    sin_t = jnp.concatenate([-sin, sin] * HEADS_PER_GROUP, axis=-1)
    return cos_t, sin_t


def _by_residue(t, dil):
    B, S, D = t.shape
    return t.reshape(B, S // dil, dil, D).transpose(0, 2, 1, 3)


def _from_residue(t):
    B, dil, L, D = t.shape
    return t.transpose(0, 2, 1, 3).reshape(B, L * dil, D)


def kernel(x, c, a_ada_w, a_ada_b, a_norm_g, a_mix_mu, a_w_in, a_w0, a_w1, a_w2, a_a0, a_a1, a_a2,
           a_k_k, a_k_a, a_r_k, a_ln_g, a_ln_b, a_w_out, kv_norm_g, w_kv, k_norm_g,
           b_ada_w, b_ada_b, b_norm_g, b_w_in, b_q_norm_g, b_w_out):
    B, S, D = x.shape
    assert a_ada_w.shape[0] == 1 and b_ada_w.shape[0] == 1
    assert D % LANES == 0 and S % (BAND_BLOCK * DIL_GROUPS[-1][1]) == 0 and S % SEQ_TILE == 0
    n_groups = len(DIL_GROUPS)
    row = lambda t: t.reshape(1, -1)
    per_head = lambda t: jnp.tile(t.reshape(1, HEAD_DIM), (1, D // HEAD_DIM))

    mod_a, mod_b = _adaln(c, a_ada_w, a_ada_b, b_ada_w, b_ada_b)
    mod_a = mod_a.reshape(B, 1, 3 * D)
    mod_b = mod_b.reshape(B, 1, 3 * D)
    cos_t, sin_t = _rope_tables(S)

    r, lw, k, v, an, bn, sg_a = _a_pre(
        x, mod_a, a_norm_g, a_mix_mu[0], a_w_in[0].astype(BF16), a_w0, a_w1[0].astype(BF16),
        a_w2[0].astype(BF16), a_a0, a_a1[0].astype(BF16), a_a2[0].astype(BF16), a_k_k, a_k_a)
    y = _wkv(r, lw, k, v, an, bn)
    xr, k_sh, v_sh = _a_post(
        y, r, k, v, sg_a, x, mod_a, a_ln_g, a_ln_b, row(a_r_k[0]), a_w_out[0].astype(BF16),
        row(kv_norm_g), w_kv.astype(BF16), per_head(k_norm_g), cos_t, sin_t)

    q, sg_b = _b_pre(xr, mod_b, b_norm_g, b_w_in[0].astype(BF16), per_head(b_q_norm_g[0]), cos_t, sin_t)

    os_, ms_, ls_ = [], [], []
    for gi, (win, dil) in enumerate(DIL_GROUPS):
        qg = _by_residue(q[:, :, gi * D:(gi + 1) * D], dil)
        o, m, l = _attn(qg, _by_residue(k_sh, dil), _by_residue(v_sh, dil), win // dil)
        os_.append(_from_residue(o))
        ms_.append(_from_residue(m))
        ls_.append(_from_residue(l))
    assert len(os_) == n_groups
    return _b_post(os_, ms_, ls_, sg_b, xr, mod_b, b_w_out[0].astype(BF16))
```

```python
import functools

import jax
import jax.numpy as jnp
from jax import lax
from jax.experimental import pallas as pl
from jax.experimental.pallas import tpu as pltpu

F32 = jnp.float32
BF16 = jnp.bfloat16

HEAD_DIM = 64
LANES = 128
HEADS_PER_GROUP = LANES // HEAD_DIM
DIL_GROUPS = ((128, 1), (512, 4), (2048, 16))
BAND_BLOCK = 128
ROPE_THETA = 10000.0
NORM_EPS = 1e-6
GN_EPS = 64e-5
NEG_INF = -1e30
CHUNK = 64
WKV_UNROLL = 4
ATTN_UNROLL = 2
SEQ_TILE = 256
VMEM_LIMIT = 56 * 1024 * 1024

_NT = (((1,), (1,)), ((), ()))
_TN = (((0,), (0,)), ((), ()))


def _dot(a, b):
    return jnp.dot(a, b, preferred_element_type=F32)


def _mm(a, b):
    return _dot(a.astype(BF16), b.astype(BF16))


def _split_dot(x, e):
    hi = x.astype(BF16)
    lo = (x - hi.astype(F32)).astype(BF16)
    return _dot(hi, e) + _dot(lo, e)


def _split_dot_lhs(e, x):
    hi = x.astype(BF16)
    lo = (x - hi.astype(F32)).astype(BF16)
    return _dot(e, hi) + _dot(e, lo)


def _mm_3pass(a, b):
    a_hi = a.astype(BF16)
    a_lo = (a - a_hi.astype(F32)).astype(BF16)
    b_hi = b.astype(BF16)
    b_lo = (b - b_hi.astype(F32)).astype(BF16)
    return _dot(a_hi, b_hi) + (_dot(a_hi, b_lo) + _dot(a_lo, b_hi))


def _head_sum(x, e):
    parts = [_split_dot(x[:, g * LANES:(g + 1) * LANES], e) for g in range(x.shape[1] // LANES)]
    return parts[0] if len(parts) == 1 else jnp.concatenate(parts, axis=1)


def _rot_half(x):
    lane = lax.broadcasted_iota(jnp.int32, (1, LANES), 1)
    first = (lane % HEAD_DIM) < (HEAD_DIM // 2)
    parts = []
    for g in range(x.shape[1] // LANES):
        xg = x[:, g * LANES:(g + 1) * LANES]
        up = pltpu.roll(xg, LANES - HEAD_DIM // 2, axis=1)
        dn = pltpu.roll(xg, HEAD_DIM // 2, axis=1)
        parts.append(jnp.where(first, up, dn))
    return parts[0] if len(parts) == 1 else jnp.concatenate(parts, axis=1)


def _tile_lanes(t, n):
    return t if n == 1 else jnp.concatenate([t] * n, axis=1)


def _rms_rows(x):
    return x * lax.rsqrt(jnp.mean(x * x, axis=-1, keepdims=True) + NORM_EPS)


def _sigmoid(x):
    return 1.0 / (1.0 + jnp.exp(-x))


def _head_ones():
    r = lax.broadcasted_iota(jnp.int32, (LANES, LANES), 0) // HEAD_DIM
    c = lax.broadcasted_iota(jnp.int32, (LANES, LANES), 1) // HEAD_DIM
    return (r == c).astype(BF16)


def _lockstep(gens):
    gens = list(gens)
    while gens:
        alive = []
        for g in gens:
            try:
                next(g)
                alive.append(g)
            except StopIteration:
                pass
        gens = alive


def _adaln_kernel(c_ref, wa_ref, ba_ref, wb_ref, bb_ref, oa_ref, ob_ref):
    c = c_ref[...]
    sc = c * _sigmoid(c)
    oa_ref[...] = _mm_3pass(sc, wa_ref[...]) + ba_ref[...]
    ob_ref[...] = _mm_3pass(sc, wb_ref[...]) + bb_ref[...]


def _adaln(c, wa, ba, wb, bb):
    B, D = c.shape
    n3 = wa.shape[-1]
    tn = 512
    wspec = pl.BlockSpec((None, D, tn), lambda j: (0, 0, j))
    bspec = pl.BlockSpec((1, tn), lambda j: (0, j))
    ospec = pl.BlockSpec((B, tn), lambda j: (0, j))
    return pl.pallas_call(
        _adaln_kernel,
        grid=(n3 // tn,),
        in_specs=[pl.BlockSpec((B, D), lambda j: (0, 0)), wspec, bspec, wspec, bspec],
        out_specs=[ospec, ospec],
        out_shape=[jax.ShapeDtypeStruct((B, n3), F32)] * 2,
        name="adaln",
    )(c, wa, ba, wb, bb)


def _a_pre_kernel(x_ref, xp_ref, mod_ref, ng_ref, mu_ref, win_ref, w0_ref, w1_ref, w2_ref,
                  a0_ref, a1_ref, a2_ref, kk_ref, ka_ref,
                  r_out, lw_out, k_out, v_out, an_out, bn_out, sg_out):
    D = x_ref.shape[-1]
    i = pl.program_id(1)
    mod = mod_ref[...]
    shift, scale = mod[:, :D], mod[:, D:2 * D]
    gain = ng_ref[...] * (1.0 + scale)

    h = _rms_rows(x_ref[...]) * gain + shift
    hp = _rms_rows(xp_ref[...]) * gain + shift
    hp = jnp.where(i == 0, 0.0, hp[7:8, :])
    row = lax.broadcasted_iota(jnp.int32, h.shape, 0)
    hs = jnp.where(row == 0, hp, pltpu.roll(h, 1, axis=0))
    xx = hs - h
    mu = mu_ref[...]
    xs = [(h + xx * mu[p:p + 1, :]).astype(BF16) for p in range(6)]

    r = _dot(xs[0], win_ref[:, 0 * D:1 * D])
    k = _dot(xs[1], win_ref[:, 1 * D:2 * D])
    v = _dot(xs[2], win_ref[:, 2 * D:3 * D])
    g = _dot(xs[3], win_ref[:, 3 * D:4 * D])
    wl = _mm(jnp.tanh(_mm(xs[4], w1_ref[...])), w2_ref[...])
    al = _mm(_mm(xs[5], a1_ref[...]), a2_ref[...])

    z = -(w0_ref[...] + wl)
    softplus = jnp.maximum(z, 0.0) + jnp.log(1.0 + jnp.exp(-jnp.abs(z)))
    w_log = -softplus - 0.5
    a = _sigmoid(a0_ref[...] + al)

    kk = k * kk_ref[...]
    ss = _head_sum(kk * kk, _head_ones())
    kk = kk / jnp.maximum(jnp.sqrt(ss), 1e-12)

    r_out[...] = r.astype(r_out.dtype)
    lw_out[...] = -jnp.exp(w_log)
    k_out[...] = (k * (1.0 + (a - 1.0) * ka_ref[...])).astype(k_out.dtype)
    v_out[...] = v.astype(v_out.dtype)
    an_out[...] = (-kk).astype(an_out.dtype)
    bn_out[...] = (kk * a).astype(bn_out.dtype)
    sg_out[...] = (g * _sigmoid(g)).astype(sg_out.dtype)


def _a_pre(x, mod3, ng, mu, w_in, w0, w1, w2, a0, a1, a2, k_k, k_a):
    B, S, D = x.shape
    ts = SEQ_TILE
    tile = pl.BlockSpec((None, ts, D), lambda b, i: (b, i, 0))
    prev = pl.BlockSpec((None, 8, D), lambda b, i: (b, jnp.maximum(i * (ts // 8) - 1, 0), 0))
    vec = pl.BlockSpec((1, D), lambda b, i: (0, 0))

    def full(a):
        return pl.BlockSpec(a.shape, lambda b, i: (0,) * a.ndim)

    act = lambda dt: jax.ShapeDtypeStruct((B, S, D), dt)
    return pl.pallas_call(
        _a_pre_kernel,
        grid=(B, S // ts),
        in_specs=[tile, prev, pl.BlockSpec((None, 1, 3 * D), lambda b, i: (b, 0, 0)), vec, full(mu),
                  full(w_in), vec, full(w1), full(w2), vec, full(a1), full(a2), vec, vec],
        out_specs=[tile] * 7,
        out_shape=[act(BF16), act(F32), act(BF16), act(BF16), act(BF16), act(BF16), act(BF16)],
        compiler_params=pltpu.CompilerParams(vmem_limit_bytes=VMEM_LIMIT),
        name="a_pre",
    )(x, x, mod3, ng, mu, w_in, w0, w1, w2, a0, a1, a2, k_k, k_a)


def _wkv_kernel(r_ref, lw_ref, k_ref, v_ref, a_ref, b_ref, y_ref,
                lp_scr, t_scr, aak_scr, ark_scr, arb_scr, ats_scr, vs_scr, rt_scr, kend_scr, bend_scr,
                wend_scr, rh_scr, y0_scr, p_scr, q_scr, st_scr):
    C = CHUNK
    n_chunks = r_ref.shape[0] // C
    W2 = 2 * C
    U = WKV_UNROLL

    lane = lax.broadcasted_iota(jnp.int32, (1, LANES), 1)
    head0 = lane < HEAD_DIM
    ri = lax.broadcasted_iota(jnp.int32, (W2, W2), 0)
    ci = lax.broadcasted_iota(jnp.int32, (W2, W2), 1)
    same = (ri // C) == (ci // C)
    strict = same & (ci < ri)
    incl = same & (ci <= ri)
    eye = (ri == ci).astype(F32)
    tri = (lax.broadcasted_iota(jnp.int32, (C, C), 1)
           <= lax.broadcasted_iota(jnp.int32, (C, C), 0)).astype(BF16)

    def stack(x):
        return jnp.concatenate([jnp.where(head0, x, 0.0), jnp.where(head0, 0.0, x)], axis=0)

    def unstack(x):
        return x[:C, :] + x[C:, :]

    def chunk_rows(c):
        return pl.ds(pl.multiple_of(c * C, C), C)

    def for_chunks(fn):
        def body(i, carry):
            _lockstep(fn(i * U + u) for u in range(U))
            return carry
        lax.fori_loop(0, n_chunks // U, body, 0)

    def gram_terms(c):
        rows = chunk_rows(c)
        lw = lw_ref[rows, :]
        r, k, v, a, b = (ref[rows, :].astype(F32) for ref in (r_ref, k_ref, v_ref, a_ref, b_ref))
        cum = _split_dot_lhs(tri, lw)
        yield
        e_cum = jnp.exp(cum)
        e_inv = jnp.exp(-cum)
        e_end = jnp.exp(cum[C - 1:C, :] - cum)
        rt = r * e_cum
        at_s = stack(a * jnp.exp(cum - lw)).astype(BF16)
        rt_s = stack(rt).astype(BF16)
        kt = (k * e_inv).astype(BF16)
        bt = (b * e_inv).astype(BF16)
        kb = jnp.concatenate([kt, kt, bt, bt], axis=0)
        ga = lax.dot_general(at_s, kb, _NT, preferred_element_type=F32)
        gr = lax.dot_general(rt_s, kb, _NT, preferred_element_type=F32)
        ats_scr[c] = at_s
        vs_scr[c] = stack(v).astype(BF16)
        rt_scr[rows, :] = rt
        kend_scr[rows, :] = (k * e_end).astype(BF16)
        bend_scr[rows, :] = (b * e_end).astype(BF16)
        wend_scr[c] = jnp.broadcast_to(e_cum[C - 1:C, :], (8, LANES))
        yield
        a_ab = jnp.where(strict, ga[:, W2:], 0.0)
        lp_scr[c] = a_ab.astype(BF16)
        t_scr[c] = eye + a_ab
        aak_scr[c] = jnp.where(strict, ga[:, :W2], 0.0).astype(BF16)
        ark_scr[c] = jnp.where(incl, gr[:, :W2], 0.0).astype(BF16)
        arb_scr[c] = jnp.where(incl, gr[:, W2:], 0.0).astype(BF16)

    def square_only(c):
        lp = lp_scr[c]
        lp2 = _dot(lp, lp)
        yield
        lp_scr[c] = lp2.astype(BF16)

    def fold_and_square(c):
        lp = lp_scr[c]
        t = t_scr[c]
        tl = _dot(t.astype(BF16), lp)
        lp2 = _dot(lp, lp)
        yield
        t_scr[c] = t + tl
        lp_scr[c] = lp2.astype(BF16)

    def fold_only(c):
        t = t_scr[c]
        tl = _dot(t.astype(BF16), lp_scr[c])
        yield
        t_scr[c] = t + tl

    def chunk_maps(c):
        rows = chunk_rows(c)
        vs = vs_scr[c]
        x = _dot(aak_scr[c], vs)
        y0a = _dot(ark_scr[c], vs)
        yield
        au = _dot(t_scr[c].astype(BF16), jnp.concatenate([ats_scr[c], x.astype(BF16)], axis=1))
        yield
        ry = _dot(arb_scr[c], au.astype(BF16))
        ah = unstack(au[:, :LANES]).astype(BF16)
        u0 = unstack(au[:, LANES:]).astype(BF16)
        k_end = kend_scr[rows, :]
        b_end = bend_scr[rows, :]
        p = lax.dot_general(b_end, ah, _TN, preferred_element_type=F32)
        q = lax.dot_general(jnp.concatenate([k_end, b_end], axis=0),
                            jnp.concatenate([v_ref[rows, :], u0], axis=0), _TN,
                            preferred_element_type=F32)
        yield
        rh_scr[rows, :] = rt_scr[rows, :] + unstack(ry[:, :LANES])
        y0_scr[rows, :] = unstack(y0a + ry[:, LANES:])
        p_scr[c] = jnp.where(same, p, 0.0) + eye * wend_scr[c][0:1, :]
        q_scr[c] = jnp.where(same, q, 0.0)

    def scan_chunks(c0):
        for u in range(U):
            c = c0 + u
            rows = chunk_rows(c)
            st = st_scr[...]
            ys = _mm(rh_scr[rows, :], st)
            st_new = _mm_3pass(p_scr[c], st)
            yield
            y_ref[rows, :] = (y0_scr[rows, :] + ys).astype(y_ref.dtype)
            st_scr[...] = st_new + q_scr[c]

    for_chunks(gram_terms)
    for_chunks(square_only)
    n = 4
    while n < C:
        for_chunks(fold_and_square)
        n *= 2
    for_chunks(fold_only)

    st_scr[...] = jnp.zeros((LANES, LANES), F32)
    _lockstep(chunk_maps(u) for u in range(U))

    def maps_and_scan(i, carry):
        _lockstep([scan_chunks((i - 1) * U)] + [chunk_maps(i * U + u) for u in range(U)])
        return carry

    lax.fori_loop(1, n_chunks // U, maps_and_scan, 0)
    _lockstep([scan_chunks(n_chunks - U)])


def _wkv(r, lw, k, v, an, bn):
    B, S, D = r.shape
    blk = pl.BlockSpec((None, S, LANES), lambda b, p: (b, 0, p))
    n_chunks = S // CHUNK
    mat = lambda dt: pltpu.VMEM((n_chunks, LANES, LANES), dt)
    seq = lambda dt: pltpu.VMEM((S, LANES), dt)
    return pl.pallas_call(
        _wkv_kernel,
        grid=(B, D // LANES),
        in_specs=[blk] * 6,
        out_specs=blk,
        out_shape=jax.ShapeDtypeStruct((B, S, D), BF16),
        scratch_shapes=[mat(BF16), mat(F32), mat(BF16), mat(BF16), mat(BF16), mat(BF16), mat(BF16),
                        seq(F32), seq(BF16), seq(BF16), pltpu.VMEM((n_chunks, 8, LANES), F32),
                        seq(F32), seq(F32), mat(F32), mat(F32),
                        pltpu.VMEM((LANES, LANES), F32)],
        compiler_params=pltpu.CompilerParams(vmem_limit_bytes=VMEM_LIMIT),
        name="wkv",
    )(r, lw, k, v, an, bn)


def _a_post_kernel(y_ref, r_ref, k_ref, v_ref, sg_ref, x_ref, mod_ref, lng_ref, lnb_ref, rk_ref,
                   wout_ref, kvg_ref, wkv_ref, kng_ref, cos_ref, sin_ref,
                   xr_out, ksh_out, vsh_out):
    D = x_ref.shape[-1]
    e = _head_ones()
    n_lane_groups = D // LANES
    f32 = lambda ref: ref[...].astype(F32)
    y = f32(y_ref)
    mean = _head_sum(y, e) * (1.0 / HEAD_DIM)
    d = y - mean
    var = _head_sum(d * d, e) * (1.0 / HEAD_DIM)
    yn = d * lax.rsqrt(var + GN_EPS) * lng_ref[...] + lnb_ref[...]
    bonus = _head_sum(f32(r_ref) * f32(k_ref) * rk_ref[...], e) * f32(v_ref)
    mix = _mm((yn + bonus) * f32(sg_ref), wout_ref[...])
    gate = mod_ref[...][:, 2 * D:]
    xr = x_ref[...] + gate * mix
    xr_out[...] = xr

    kv = _mm(_rms_rows(xr) * kvg_ref[...], wkv_ref[...])
    ks = kv[:, :D]
    ks = ks * lax.rsqrt(_head_sum(ks * ks, e) * (1.0 / HEAD_DIM) + NORM_EPS) * kng_ref[...]
    cos = _tile_lanes(cos_ref[...], n_lane_groups)
    sin = _tile_lanes(sin_ref[...], n_lane_groups)
    ksh_out[...] = (ks * cos + _rot_half(ks) * sin).astype(ksh_out.dtype)
    vsh_out[...] = kv[:, D:].astype(vsh_out.dtype)


def _a_post(y, r, k, v, sg, x, mod3, ln_g, ln_b, r_k, w_out, kv_g, w_kv, kn_g, cos_t, sin_t):
    B, S, D = x.shape
    ts = SEQ_TILE
    tile = pl.BlockSpec((None, ts, D), lambda b, i: (b, i, 0))
    vec = pl.BlockSpec((1, D), lambda b, i: (0, 0))
    rope = pl.BlockSpec((ts, LANES), lambda b, i: (i, 0))

    def full(a):
        return pl.BlockSpec(a.shape, lambda b, i: (0,) * a.ndim)

    act = lambda dt: jax.ShapeDtypeStruct((B, S, D), dt)
    return pl.pallas_call(
        _a_post_kernel,
        grid=(B, S // ts),
        in_specs=[tile] * 6 + [pl.BlockSpec((None, 1, 3 * D), lambda b, i: (b, 0, 0)), vec, vec, vec,
                               full(w_out), vec, full(w_kv), vec, rope, rope],
        out_specs=[tile] * 3,
        out_shape=[act(F32), act(BF16), act(BF16)],
        compiler_params=pltpu.CompilerParams(vmem_limit_bytes=VMEM_LIMIT),
        name="a_post",
    )(y, r, k, v, sg, x, mod3, ln_g, ln_b, r_k, w_out, kv_g, w_kv, kn_g, cos_t, sin_t)


def _b_pre_kernel(x_ref, mod_ref, ng_ref, win_ref, qg_ref, cos_ref, sin_ref, q_out, sg_out):
    D = x_ref.shape[-1]
    nq = q_out.shape[-1]
    e = _head_ones()
    mod = mod_ref[...]
    shift, scale = mod[:, :D], mod[:, D:2 * D]
    h = (_rms_rows(x_ref[...]) * (ng_ref[...] * (1.0 + scale)) + shift).astype(BF16)
    cos = _tile_lanes(cos_ref[...], D // LANES)
    sin = _tile_lanes(sin_ref[...], D // LANES)
    for g in range(nq // D):
        q = _dot(h, win_ref[:, g * D:(g + 1) * D])
        q = q * lax.rsqrt(_head_sum(q * q, e) * (1.0 / HEAD_DIM) + NORM_EPS) * qg_ref[...]
        q_out[:, g * D:(g + 1) * D] = ((q * cos + _rot_half(q) * sin) * (HEAD_DIM ** -0.5)).astype(q_out.dtype)
    gate = _dot(h, win_ref[:, nq:])
    sg_out[...] = (gate * _sigmoid(gate)).astype(sg_out.dtype)


def _b_pre(xr, mod3, ng, w_in, qn_g, cos_t, sin_t):
    B, S, D = xr.shape
    nq = w_in.shape[1] - D
    ts = SEQ_TILE
    tile = pl.BlockSpec((None, ts, D), lambda b, i: (b, i, 0))
    vec = pl.BlockSpec((1, D), lambda b, i: (0, 0))
    rope = pl.BlockSpec((ts, LANES), lambda b, i: (i, 0))
    return pl.pallas_call(
        _b_pre_kernel,
        grid=(B, S // ts),
        in_specs=[tile, pl.BlockSpec((None, 1, 3 * D), lambda b, i: (b, 0, 0)), vec,
                  pl.BlockSpec(w_in.shape, lambda b, i: (0, 0)), vec, rope, rope],
        out_specs=[pl.BlockSpec((None, ts, nq), lambda b, i: (b, i, 0)), tile],
        out_shape=[jax.ShapeDtypeStruct((B, S, nq), BF16), jax.ShapeDtypeStruct((B, S, D), BF16)],
        compiler_params=pltpu.CompilerParams(vmem_limit_bytes=VMEM_LIMIT),
        name="b_pre",
    )(xr, mod3, ng, w_in, qn_g, cos_t, sin_t)


def _attn_kernel(q0_ref, q1_ref, q2_ref, k_ref, v_ref, o_ref,
                 tmp_scr, qr_scr, kr_scr, vr_scr, acc_scr, m_scr, l_scr):
    S = k_ref.shape[0]
    blk = BAND_BLOCK
    lane = lax.broadcasted_iota(jnp.int32, (1, LANES), 1)
    head0 = lane < HEAD_DIM

    for gi, (_, dil) in enumerate(DIL_GROUPS):
        if dil == 1:
            continue
        seg = S // dil
        for src_ref, dst_scr in ((q1_ref if gi == 1 else q2_ref, qr_scr), (k_ref, kr_scr), (v_ref, vr_scr)):
            tmp_scr[...] = src_ref[...].astype(F32)
            for rho in range(dil):
                dst_scr[gi - 1, rho * seg:(rho + 1) * seg, :] = (
                    tmp_scr[pl.ds(rho, seg, stride=dil), :].astype(BF16))

    def block(gi, rho, n):
        win, dil = DIL_GROUPS[gi]
        win_sub = win // dil
        nb = S // dil // blk
        j = rho * nb + n
        if gi == 0:
            q_src, k_src, v_src = q0_ref, k_ref, v_ref
        else:
            q_src, k_src, v_src = qr_scr.at[gi - 1], kr_scr.at[gi - 1], vr_scr.at[gi - 1]
        lo = (j - 1) * blk if n > 0 else j * blk
        nk = (j + 1) * blk - lo
        q = q_src[j * blk:(j + 1) * blk, :]
        kc = k_src[lo:lo + nk, :]
        vc = v_src[lo:lo + nk, :]
        qi = lax.broadcasted_iota(jnp.int32, (blk, nk), 0)
        kj = lax.broadcasted_iota(jnp.int32, (blk, nk), 1)
        diff = (nk - blk) + qi - kj
        valid = (diff >= 0) & (diff <= win_sub)
        zero = jnp.zeros_like(q)
        s = [lax.dot_general(jnp.where(head0, q, zero) if h == 0 else jnp.where(head0, zero, q), kc, _NT,
                             preferred_element_type=F32) for h in range(HEADS_PER_GROUP)]
        yield
        ms, ls, pv = [], [], []
        for h in range(HEADS_PER_GROUP):
            sh = jnp.where(valid, s[h], NEG_INF)
            m = jnp.max(sh, axis=-1, keepdims=True)
            p = jnp.exp(sh - m)
            ms.append(m)
            ls.append(jnp.sum(p, axis=-1, keepdims=True))
            pv.append(_dot(p.astype(BF16), vc))
        yield
        if dil == 1:
            rows = slice(j * blk, (j + 1) * blk)
        else:
            rows = pl.ds(rho + dil * blk * n, blk, stride=dil)
        acc_scr[gi, rows, :] = jnp.where(head0, pv[0], pv[1])
        m_scr[gi, rows, :] = jnp.where(head0, ms[0], ms[1])
        l_scr[gi, rows, :] = jnp.where(head0, ls[0], ls[1])

    blocks = [(gi, rho, n) for gi, (_, dil) in enumerate(DIL_GROUPS)
              for rho in range(dil) for n in range(S // dil // blk)]
    for i in range(0, len(blocks), ATTN_UNROLL):
        _lockstep(block(*b) for b in blocks[i:i + ATTN_UNROLL])

    def merge(t, carry):
        rows = pl.ds(pl.multiple_of(t * blk, blk), blk)
        ms = [m_scr[g, rows, :] for g in range(len(DIL_GROUPS))]
        m_all = functools.reduce(jnp.maximum, ms)
        ws = [jnp.exp(m - m_all) for m in ms]
        num = functools.reduce(jnp.add, [w * acc_scr[g, rows, :] for g, w in enumerate(ws)])
        den = functools.reduce(jnp.add, [w * l_scr[g, rows, :] for g, w in enumerate(ws)])
        o_ref[rows, :] = (num / den).astype(o_ref.dtype)
        return carry

    lax.fori_loop(0, S // blk, merge, 0)


def _attn(q, k_sh, v_sh):
    B, S, D = k_sh.shape
    n_groups = len(DIL_GROUPS)
    n_pairs = D // LANES
    qspec = lambda g: pl.BlockSpec((None, S, LANES), lambda b, p: (b, 0, g * n_pairs + p))
    blk = pl.BlockSpec((None, S, LANES), lambda b, p: (b, 0, p))
    res = lambda: pltpu.VMEM((n_groups - 1, S, LANES), BF16)
    nat = lambda: pltpu.VMEM((n_groups, S, LANES), F32)
    return pl.pallas_call(
        _attn_kernel,
        grid=(B, n_pairs),
        in_specs=[qspec(0), qspec(1), qspec(2), blk, blk],
        out_specs=blk,
        out_shape=jax.ShapeDtypeStruct((B, S, D), BF16),
        scratch_shapes=[pltpu.VMEM((S, LANES), F32), res(), res(), res(), nat(), nat(), nat()],
        compiler_params=pltpu.CompilerParams(vmem_limit_bytes=VMEM_LIMIT),
        name="attn",
    )(q, q, q, k_sh, v_sh)


def _b_post_kernel(att_ref, sg_ref, x_ref, mod_ref, wout_ref, out_ref):
    D = x_ref.shape[-1]
    gate = mod_ref[...][:, 2 * D:]
    y = att_ref[...].astype(F32) * sg_ref[...].astype(F32)
    out_ref[...] = x_ref[...] + gate * _mm(y, wout_ref[...])


def _b_post(att, sg, xr, mod3, w_out):
    B, S, D = xr.shape
    ts = SEQ_TILE
    tile = pl.BlockSpec((None, ts, D), lambda b, i: (b, i, 0))
    return pl.pallas_call(
        _b_post_kernel,
        grid=(B, S // ts),
        in_specs=[tile, tile, tile, pl.BlockSpec((None, 1, 3 * D), lambda b, i: (b, 0, 0)),
                  pl.BlockSpec(w_out.shape, lambda b, i: (0, 0))],
        out_specs=tile,
        out_shape=jax.ShapeDtypeStruct((B, S, D), F32),
        compiler_params=pltpu.CompilerParams(vmem_limit_bytes=VMEM_LIMIT),
        name="b_post",
    )(att, sg, xr, mod3, w_out)


def _rope_tables(seq):
    pos = jnp.arange(seq, dtype=F32)
    inv = ROPE_THETA ** (-jnp.arange(0, HEAD_DIM, 2, dtype=F32) / HEAD_DIM)
    ang = pos[:, None] * inv[None, :]
    cos, sin = jnp.cos(ang), jnp.sin(ang)
    cos_t = jnp.concatenate([cos, cos] * HEADS_PER_GROUP, axis=-1)
    sin_t = jnp.concatenate([-sin, sin] * HEADS_PER_GROUP, axis=-1)
    return cos_t, sin_t


def kernel(x, c, a_ada_w, a_ada_b, a_norm_g, a_mix_mu, a_w_in, a_w0, a_w1, a_w2, a_a0, a_a1, a_a2,
           a_k_k, a_k_a, a_r_k, a_ln_g, a_ln_b, a_w_out, kv_norm_g, w_kv, k_norm_g,
           b_ada_w, b_ada_b, b_norm_g, b_w_in, b_q_norm_g, b_w_out):
    B, S, D = x.shape
    assert a_ada_w.shape[0] == 1 and b_ada_w.shape[0] == 1
    assert D % LANES == 0 and S % (BAND_BLOCK * DIL_GROUPS[-1][1]) == 0 and S % SEQ_TILE == 0
    assert all(win // dil == BAND_BLOCK for win, dil in DIL_GROUPS)
    assert b_w_in.shape[-1] == (len(DIL_GROUPS) + 1) * D
    row = lambda t: t.reshape(1, -1)
    per_head = lambda t: jnp.tile(t.reshape(1, HEAD_DIM), (1, D // HEAD_DIM))

    mod_a, mod_b = _adaln(c, a_ada_w, a_ada_b, b_ada_w, b_ada_b)
    mod_a = mod_a.reshape(B, 1, 3 * D)
    mod_b = mod_b.reshape(B, 1, 3 * D)
    cos_t, sin_t = _rope_tables(S)

    r, lw, k, v, an, bn, sg_a = _a_pre(
        x, mod_a, a_norm_g, a_mix_mu[0], a_w_in[0].astype(BF16), a_w0, a_w1[0].astype(BF16),
        a_w2[0].astype(BF16), a_a0, a_a1[0].astype(BF16), a_a2[0].astype(BF16), a_k_k, a_k_a)
    y = _wkv(r, lw, k, v, an, bn)
    xr, k_sh, v_sh = _a_post(
        y, r, k, v, sg_a, x, mod_a, a_ln_g, a_ln_b, row(a_r_k[0]), a_w_out[0].astype(BF16),
        row(kv_norm_g), w_kv.astype(BF16), per_head(k_norm_g), cos_t, sin_t)

    q, sg_b = _b_pre(xr, mod_b, b_norm_g, b_w_in[0].astype(BF16), per_head(b_q_norm_g[0]), cos_t, sin_t)
    att = _attn(q, k_sh, v_sh)
    return _b_post(att, sg_b, xr, mod_b, b_w_out[0].astype(BF16))
```

```python
import functools

import jax
import jax.numpy as jnp
from jax import lax
from jax.experimental import pallas as pl
from jax.experimental.pallas import tpu as pltpu

F32 = jnp.float32
BF16 = jnp.bfloat16

HEAD_DIM = 64
LANES = 128
HEADS_PER_GROUP = LANES // HEAD_DIM
DIL_GROUPS = ((128, 1), (512, 4), (2048, 16))
BAND_BLOCK = 128
ROPE_THETA = 10000.0
NORM_EPS = 1e-6
GN_EPS = 64e-5
NEG_INF = -1e30
DECAY_SCALE = 0.6065306597126334
CHUNK = 64
WKV_UNROLL = 8
ATTN_BATCH = 4
SEQ_TILE = 256
VMEM_LIMIT = 56 * 1024 * 1024

_NT = (((1,), (1,)), ((), ()))
_TN = (((0,), (0,)), ((), ()))


def _dot(a, b):
    return jnp.dot(a, b, preferred_element_type=F32)


def _mm(a, b):
    return _dot(a.astype(BF16), b.astype(BF16))


def _split_dot_lhs(e, x):
    hi = x.astype(BF16)
    lo = (x - hi.astype(F32)).astype(BF16)
    return _dot(e, hi) + _dot(e, lo)


def _mm_3pass(a, b):
    a_hi = a.astype(BF16)
    a_lo = (a - a_hi.astype(F32)).astype(BF16)
    b_hi = b.astype(BF16)
    b_lo = (b - b_hi.astype(F32)).astype(BF16)
    return _dot(a_hi, b_hi) + (_dot(a_hi, b_lo) + _dot(a_lo, b_hi))


def _head_sum(x, e):
    parts = [_dot(x[:, g * LANES:(g + 1) * LANES].astype(BF16), e) for g in range(x.shape[1] // LANES)]
    return parts[0] if len(parts) == 1 else jnp.concatenate(parts, axis=1)


def _rot_half(x):
    lane = lax.broadcasted_iota(jnp.int32, (1, LANES), 1)
    first = (lane % HEAD_DIM) < (HEAD_DIM // 2)
    parts = []
    for g in range(x.shape[1] // LANES):
        xg = x[:, g * LANES:(g + 1) * LANES]
        up = pltpu.roll(xg, LANES - HEAD_DIM // 2, axis=1)
        dn = pltpu.roll(xg, HEAD_DIM // 2, axis=1)
        parts.append(jnp.where(first, up, dn))
    return parts[0] if len(parts) == 1 else jnp.concatenate(parts, axis=1)


def _tile_lanes(t, n):
    return t if n == 1 else jnp.concatenate([t] * n, axis=1)


def _rms_rows(x):
    return x * lax.rsqrt(jnp.mean(x * x, axis=-1, keepdims=True) + NORM_EPS)


def _sigmoid(x):
    return 1.0 / (1.0 + jnp.exp(-x))


def _head_ones():
    r = lax.broadcasted_iota(jnp.int32, (LANES, LANES), 0) // HEAD_DIM
    c = lax.broadcasted_iota(jnp.int32, (LANES, LANES), 1) // HEAD_DIM
    return (r == c).astype(BF16)


def _lockstep(gens):
    gens = list(gens)
    while gens:
        alive = []
        for g in gens:
            try:
                next(g)
                alive.append(g)
            except StopIteration:
                pass
        gens = alive


def _adaln_kernel(c_ref, wa_ref, ba_ref, wb_ref, bb_ref, oa_ref, ob_ref):
    c = c_ref[...]
    sc = c * _sigmoid(c)
    oa_ref[...] = _mm_3pass(sc, wa_ref[...]) + ba_ref[...]
    ob_ref[...] = _mm_3pass(sc, wb_ref[...]) + bb_ref[...]


def _adaln(c, wa, ba, wb, bb):
    B, D = c.shape
    n3 = wa.shape[-1]
    tn = 512
    wspec = pl.BlockSpec((None, D, tn), lambda j: (0, 0, j))
    bspec = pl.BlockSpec((1, tn), lambda j: (0, j))
    ospec = pl.BlockSpec((B, tn), lambda j: (0, j))
    return pl.pallas_call(
        _adaln_kernel,
        grid=(n3 // tn,),
        in_specs=[pl.BlockSpec((B, D), lambda j: (0, 0)), wspec, bspec, wspec, bspec],
        out_specs=[ospec, ospec],
        out_shape=[jax.ShapeDtypeStruct((B, n3), F32)] * 2,
        name="adaln",
    )(c, wa, ba, wb, bb)


def _a_pre_kernel(x_ref, xp_ref, mod_ref, ng_ref, mu_ref, win_ref, w0_ref, w1_ref, w2_ref,
                  a0_ref, a1_ref, a2_ref, kk_ref, ka_ref,
                  r_out, lw_out, k_out, v_out, an_out, bn_out, sg_out):
    D = x_ref.shape[-1]
    i = pl.program_id(1)
    mod = mod_ref[...]
    shift, scale = mod[:, :D], mod[:, D:2 * D]
    gain = ng_ref[...] * (1.0 + scale)

    h = _rms_rows(x_ref[...]) * gain + shift
    hp = _rms_rows(xp_ref[...]) * gain + shift
    hp = jnp.where(i == 0, 0.0, hp[7:8, :])
    row = lax.broadcasted_iota(jnp.int32, h.shape, 0)
    hs = jnp.where(row == 0, hp, pltpu.roll(h, 1, axis=0))
    xx = hs - h
    mu = mu_ref[...]
    xs = [(h + xx * mu[p:p + 1, :]).astype(BF16) for p in range(6)]

    r = _dot(xs[0], win_ref[:, 0 * D:1 * D])
    k = _dot(xs[1], win_ref[:, 1 * D:2 * D])
    v = _dot(xs[2], win_ref[:, 2 * D:3 * D])
    g = _dot(xs[3], win_ref[:, 3 * D:4 * D])
    wl = _mm(jnp.tanh(_mm(xs[4], w1_ref[...])), w2_ref[...])
    al = _mm(_mm(xs[5], a1_ref[...]), a2_ref[...])

    a = _sigmoid(a0_ref[...] + al)

    kk = k * kk_ref[...]
    ss = _head_sum(kk * kk, _head_ones())
    kk = kk * lax.rsqrt(jnp.maximum(ss, 1e-24))

    r_out[...] = r.astype(r_out.dtype)
    lw_out[...] = (-DECAY_SCALE) * _sigmoid(w0_ref[...] + wl)
    k_out[...] = (k * (1.0 + (a - 1.0) * ka_ref[...])).astype(k_out.dtype)
    v_out[...] = v.astype(v_out.dtype)
    an_out[...] = (-kk).astype(an_out.dtype)
    bn_out[...] = (kk * a).astype(bn_out.dtype)
    sg_out[...] = (g * _sigmoid(g)).astype(sg_out.dtype)


def _a_pre(x, mod3, ng, mu, w_in, w0, w1, w2, a0, a1, a2, k_k, k_a):
    B, S, D = x.shape
    ts = SEQ_TILE
    tile = pl.BlockSpec((None, ts, D), lambda b, i: (b, i, 0))
    prev = pl.BlockSpec((None, 8, D), lambda b, i: (b, jnp.maximum(i * (ts // 8) - 1, 0), 0))
    vec = pl.BlockSpec((1, D), lambda b, i: (0, 0))

    def full(a):
        return pl.BlockSpec(a.shape, lambda b, i: (0,) * a.ndim)

    act = lambda dt: jax.ShapeDtypeStruct((B, S, D), dt)
    return pl.pallas_call(
        _a_pre_kernel,
        grid=(B, S // ts),
        in_specs=[tile, prev, pl.BlockSpec((None, 1, 3 * D), lambda b, i: (b, 0, 0)), vec, full(mu),
                  full(w_in), vec, full(w1), full(w2), vec, full(a1), full(a2), vec, vec],
        out_specs=[tile] * 7,
        out_shape=[act(BF16), act(F32), act(BF16), act(BF16), act(BF16), act(BF16), act(BF16)],
        compiler_params=pltpu.CompilerParams(vmem_limit_bytes=VMEM_LIMIT),
        name="a_pre",
    )(x, x, mod3, ng, mu, w_in, w0, w1, w2, a0, a1, a2, k_k, k_a)


def _wkv_kernel(r_ref, lw_ref, k_ref, v_ref, a_ref, b_ref, y_ref,
                lp_scr, t_scr, aak_scr, ark_scr, arb_scr, ats_scr, vs_scr, rt_scr, kend_scr, bend_scr,
                wend_scr, rh_scr, y0_scr, p_scr, q_scr, st_scr):
    C = CHUNK
    n_chunks = r_ref.shape[0] // C
    U = WKV_UNROLL
    assert C == HEAD_DIM

    lane = lax.broadcasted_iota(jnp.int32, (1, LANES), 1)
    head0 = lane < HEAD_DIM
    ti = lax.broadcasted_iota(jnp.int32, (C, LANES), 0)
    ii = lax.broadcasted_iota(jnp.int32, (C, LANES), 1) % C
    strict = ii < ti
    incl = ii <= ti
    eye_pair = (ii == ti).astype(F32)
    ri = lax.broadcasted_iota(jnp.int32, (LANES, LANES), 0)
    ci = lax.broadcasted_iota(jnp.int32, (LANES, LANES), 1)
    same = (ri // HEAD_DIM) == (ci // HEAD_DIM)
    eye = (ri == ci).astype(F32)
    tri = (lax.broadcasted_iota(jnp.int32, (C, C), 1)
           <= lax.broadcasted_iota(jnp.int32, (C, C), 0)).astype(BF16)

    def stack(x):
        zero = jnp.zeros_like(x)
        return jnp.concatenate([jnp.where(head0, x, zero), jnp.where(head0, zero, x)], axis=0)

    def chunk_rows(c):
        return pl.ds(pl.multiple_of(c * C, C), C)

    def gram_terms(c):
        rows = chunk_rows(c)
        lw = lw_ref[rows, :]
        r, k, v, a, b = (ref[rows, :].astype(F32) for ref in (r_ref, k_ref, v_ref, a_ref, b_ref))
        cum = _split_dot_lhs(tri, lw)
        yield
        e_cum = jnp.exp(cum)
        e_inv = jnp.exp(-cum)
        e_end = jnp.exp(cum[C - 1:C, :] - cum)
        rt = r * e_cum
        at = (a * jnp.exp(cum - lw)).astype(BF16)
        kt = (k * e_inv).astype(BF16)
        bt = (b * e_inv).astype(BF16)
        kb = jnp.concatenate([stack(kt), stack(bt)], axis=0)
        ga = lax.dot_general(at, kb, _NT, preferred_element_type=F32)
        gr = lax.dot_general(rt.astype(BF16), kb, _NT, preferred_element_type=F32)
        ats_scr[c] = stack(at)
        vs_scr[c] = stack(v.astype(BF16))
        rt_scr[rows, :] = rt
        kend_scr[rows, :] = (k * e_end).astype(BF16)
        bend_scr[rows, :] = (b * e_end).astype(BF16)
        wend_scr[c] = jnp.broadcast_to(e_cum[C - 1:C, :], (8, LANES))
        yield
        a_ab = jnp.where(strict, ga[:, LANES:], 0.0)
        lp_scr[c] = a_ab.astype(BF16)
        t_scr[c] = eye_pair + a_ab
        aak_scr[c] = jnp.where(strict, ga[:, :LANES], 0.0).astype(BF16)
        ark_scr[c] = jnp.where(incl, gr[:, :LANES], 0.0).astype(BF16)
        arb_scr[c] = jnp.where(incl, gr[:, LANES:], 0.0).astype(BF16)

    def square_only(c):
        lp = lp_scr[c]
        lp2 = _dot(lp, stack(lp))
        yield
        lp_scr[c] = lp2.astype(BF16)

    def fold_and_square(c):
        lp = lp_scr[c]
        lp_bd = stack(lp)
        t = t_scr[c]
        tl = _dot(t.astype(BF16), lp_bd)
        lp2 = _dot(lp, lp_bd)
        yield
        t_scr[c] = t + tl
        lp_scr[c] = lp2.astype(BF16)

    def fold_only(c):
        t = t_scr[c]
        tl = _dot(t.astype(BF16), stack(lp_scr[c]))
        yield
        t_scr[c] = t + tl

    def chunk_maps(c):
        rows = chunk_rows(c)
        vs = vs_scr[c]
        x = _dot(aak_scr[c], vs)
        y0a = _dot(ark_scr[c], vs)
        yield
        au = _dot(t_scr[c].astype(BF16),
                  jnp.concatenate([ats_scr[c], stack(x.astype(BF16))], axis=1))
        yield
        ah = au[:, :LANES].astype(BF16)
        u0 = au[:, LANES:].astype(BF16)
        ry = _dot(arb_scr[c], jnp.concatenate([stack(ah), stack(u0)], axis=1))
        k_end = kend_scr[rows, :]
        b_end = bend_scr[rows, :]
        p = lax.dot_general(b_end, ah, _TN, preferred_element_type=F32)
        q = lax.dot_general(jnp.concatenate([k_end, b_end], axis=0),
                            jnp.concatenate([v_ref[rows, :], u0], axis=0), _TN,
                            preferred_element_type=F32)
        yield
        rh_scr[rows, :] = rt_scr[rows, :] + ry[:, :LANES]
        y0_scr[rows, :] = y0a + ry[:, LANES:]
        p_scr[c] = jnp.where(same, p, 0.0) + eye * wend_scr[c][0:1, :]
        q_scr[c] = jnp.where(same, q, 0.0)

    def scan_chunk(c):
        rows = chunk_rows(c)
        st = st_scr[...]
        ys = _mm(rh_scr[rows, :], st)
        st_new = _mm_3pass(p_scr[c], st)
        yield
        y_ref[rows, :] = (y0_scr[rows, :] + ys).astype(y_ref.dtype)
        st_scr[...] = st_new + q_scr[c]

    stages = [gram_terms, square_only]
    n = 4
    while n < C:
        stages.append(fold_and_square)
        n *= 2
    stages += [fold_only, chunk_maps]
    assert len(stages) == U and n_chunks % U == 0

    def group(g, scan_prev):
        for s, stage in enumerate(stages):
            gens = [stage(g * U + u) for u in range(U)]
            if scan_prev:
                gens = [scan_chunk((g - 1) * U + s)] + gens
            _lockstep(gens)

    st_scr[...] = jnp.zeros((LANES, LANES), F32)
    group(0, False)

    def group_and_scan(g, carry):
        group(g, True)
        return carry

    lax.fori_loop(1, n_chunks // U, group_and_scan, 0)
    for s in range(U):
        _lockstep([scan_chunk(n_chunks - U + s)])


def _wkv(r, lw, k, v, an, bn):
    B, S, D = r.shape
    blk = pl.BlockSpec((None, S, LANES), lambda b, p: (b, 0, p))
    n_chunks = S // CHUNK
    mat = lambda dt: pltpu.VMEM((n_chunks, LANES, LANES), dt)
    pair = lambda dt: pltpu.VMEM((n_chunks, CHUNK, LANES), dt)
    seq = lambda dt: pltpu.VMEM((S, LANES), dt)
    return pl.pallas_call(
        _wkv_kernel,
        grid=(B, D // LANES),
        in_specs=[blk] * 6,
        out_specs=blk,
        out_shape=jax.ShapeDtypeStruct((B, S, D), BF16),
        scratch_shapes=[pair(BF16), pair(F32), pair(BF16), pair(BF16), pair(BF16), mat(BF16), mat(BF16),
                        seq(F32), seq(BF16), seq(BF16), pltpu.VMEM((n_chunks, 8, LANES), F32),
                        seq(F32), seq(F32), mat(F32), mat(F32),
                        pltpu.VMEM((LANES, LANES), F32)],
        compiler_params=pltpu.CompilerParams(vmem_limit_bytes=VMEM_LIMIT),
        name="wkv",
    )(r, lw, k, v, an, bn)


def _a_post_kernel(y_ref, r_ref, k_ref, v_ref, sg_ref, x_ref, mod_ref, lng_ref, lnb_ref, rk_ref,
                   wout_ref, kvg_ref, wkv_ref, kng_ref, cos_ref, sin_ref,
                   xr_out, ksh_out, vsh_out):
    D = x_ref.shape[-1]
    e = _head_ones()
    n_lane_groups = D // LANES
    f32 = lambda ref: ref[...].astype(F32)
    y = f32(y_ref)
    mean = _head_sum(y, e) * (1.0 / HEAD_DIM)
    d = y - mean
    var = _head_sum(d * d, e) * (1.0 / HEAD_DIM)
    yn = d * lax.rsqrt(var + GN_EPS) * lng_ref[...] + lnb_ref[...]
    bonus = _head_sum(f32(r_ref) * f32(k_ref) * rk_ref[...], e) * f32(v_ref)
    mix = _mm((yn + bonus) * f32(sg_ref), wout_ref[...])
    gate = mod_ref[...][:, 2 * D:]
    xr = x_ref[...] + gate * mix
    xr_out[...] = xr

    kv = _mm(_rms_rows(xr) * kvg_ref[...], wkv_ref[...])
    ks = kv[:, :D]
    ks = ks * lax.rsqrt(_head_sum(ks * ks, e) * (1.0 / HEAD_DIM) + NORM_EPS) * kng_ref[...]
    cos = _tile_lanes(cos_ref[...], n_lane_groups)
    sin = _tile_lanes(sin_ref[...], n_lane_groups)
    ksh_out[...] = (ks * cos + _rot_half(ks) * sin).astype(ksh_out.dtype)
    vsh_out[...] = kv[:, D:].astype(vsh_out.dtype)


def _a_post(y, r, k, v, sg, x, mod3, ln_g, ln_b, r_k, w_out, kv_g, w_kv, kn_g, cos_t, sin_t):
    B, S, D = x.shape
    ts = SEQ_TILE
    tile = pl.BlockSpec((None, ts, D), lambda b, i: (b, i, 0))
    vec = pl.BlockSpec((1, D), lambda b, i: (0, 0))
    rope = pl.BlockSpec((ts, LANES), lambda b, i: (i, 0))

    def full(a):
        return pl.BlockSpec(a.shape, lambda b, i: (0,) * a.ndim)

    act = lambda dt: jax.ShapeDtypeStruct((B, S, D), dt)
    return pl.pallas_call(
        _a_post_kernel,
        grid=(B, S // ts),
        in_specs=[tile] * 6 + [pl.BlockSpec((None, 1, 3 * D), lambda b, i: (b, 0, 0)), vec, vec, vec,
                               full(w_out), vec, full(w_kv), vec, rope, rope],
        out_specs=[tile] * 3,
        out_shape=[act(F32), act(BF16), act(BF16)],
        compiler_params=pltpu.CompilerParams(vmem_limit_bytes=VMEM_LIMIT),
        name="a_post",
    )(y, r, k, v, sg, x, mod3, ln_g, ln_b, r_k, w_out, kv_g, w_kv, kn_g, cos_t, sin_t)


def _b_pre_kernel(x_ref, mod_ref, ng_ref, win_ref, qg_ref, cos_ref, sin_ref, q_out, sg_out):
    D = x_ref.shape[-1]
    nq = q_out.shape[-1]
    e = _head_ones()
    mod = mod_ref[...]
    shift, scale = mod[:, :D], mod[:, D:2 * D]
    h = (_rms_rows(x_ref[...]) * (ng_ref[...] * (1.0 + scale)) + shift).astype(BF16)
    cos = _tile_lanes(cos_ref[...], D // LANES)
    sin = _tile_lanes(sin_ref[...], D // LANES)
    for g in range(nq // D):
        q = _dot(h, win_ref[:, g * D:(g + 1) * D])
        q = q * lax.rsqrt(_head_sum(q * q, e) * (1.0 / HEAD_DIM) + NORM_EPS) * qg_ref[...]
        q_out[:, g * D:(g + 1) * D] = ((q * cos + _rot_half(q) * sin) * (HEAD_DIM ** -0.5)).astype(q_out.dtype)
    gate = _dot(h, win_ref[:, nq:])
    sg_out[...] = (gate * _sigmoid(gate)).astype(sg_out.dtype)


def _b_pre(xr, mod3, ng, w_in, qn_g, cos_t, sin_t):
    B, S, D = xr.shape
    nq = w_in.shape[1] - D
    ts = SEQ_TILE
    tile = pl.BlockSpec((None, ts, D), lambda b, i: (b, i, 0))
    vec = pl.BlockSpec((1, D), lambda b, i: (0, 0))
    rope = pl.BlockSpec((ts, LANES), lambda b, i: (i, 0))
    return pl.pallas_call(
        _b_pre_kernel,
        grid=(B, S // ts),
        in_specs=[tile, pl.BlockSpec((None, 1, 3 * D), lambda b, i: (b, 0, 0)), vec,
                  pl.BlockSpec(w_in.shape, lambda b, i: (0, 0)), vec, rope, rope],
        out_specs=[pl.BlockSpec((None, ts, nq), lambda b, i: (b, i, 0)), tile],
        out_shape=[jax.ShapeDtypeStruct((B, S, nq), BF16), jax.ShapeDtypeStruct((B, S, D), BF16)],
        compiler_params=pltpu.CompilerParams(vmem_limit_bytes=VMEM_LIMIT),
        name="b_pre",
    )(xr, mod3, ng, w_in, qn_g, cos_t, sin_t)


def _attn_kernel(q0_ref, q1_ref, q2_ref, k_ref, v_ref, o_ref,
                 tmp_scr, qr_scr, kr_scr, vr_scr, acc_scr, m_scr, l_scr, s_scr, p_scr):
    S = k_ref.shape[0]
    blk = BAND_BLOCK
    lane = lax.broadcasted_iota(jnp.int32, (1, LANES), 1)
    head0 = lane < HEAD_DIM

    def head_ones(nk):
        hsel = (lax.broadcasted_iota(jnp.int32, (HEADS_PER_GROUP * nk, LANES), 1) // HEAD_DIM
                == lax.broadcasted_iota(jnp.int32, (HEADS_PER_GROUP * nk, LANES), 0) // nk)
        return hsel.astype(F32).astype(BF16)

    def upcast(src_ref):
        def tile(t, carry):
            rows = pl.ds(pl.multiple_of(t * blk, blk), blk)
            tmp_scr[rows, :] = src_ref[rows, :].astype(F32)
            return carry
        lax.fori_loop(0, S // blk, tile, 0)

    for gi, (_, dil) in enumerate(DIL_GROUPS):
        if dil == 1:
            continue
        seg = S // dil
        for src_ref, dst_scr in ((q1_ref if gi == 1 else q2_ref, qr_scr), (k_ref, kr_scr), (v_ref, vr_scr)):
            upcast(src_ref)
            for rho in range(dil):
                dst_scr[gi - 1, rho * seg:(rho + 1) * seg, :] = (
                    tmp_scr[pl.ds(rho, seg, stride=dil), :].astype(BF16))

    def geometry(gi, rho, n):
        dil = DIL_GROUPS[gi][1]
        j = rho * (S // dil // blk) + n
        lo = (j - 1) * blk if n > 0 else j * blk
        nk = (j + 1) * blk - lo
        if gi == 0:
            srcs = (q0_ref, k_ref, v_ref)
        else:
            srcs = (qr_scr.at[gi - 1], kr_scr.at[gi - 1], vr_scr.at[gi - 1])
        return j, lo, nk, srcs

    def token_rows(gi, rho, n, start, size):
        dil = DIL_GROUPS[gi][1]
        first = rho + dil * (blk * n + start)
        return slice(first, first + size) if dil == 1 else pl.ds(first, size, stride=dil)

    def per_head(x, other):
        return jnp.concatenate([jnp.where(head0, x, other), jnp.where(head0, other, x)], axis=0)

    def scores(b, slot):
        j, lo, nk, (q_src, k_src, _) = geometry(*b)
        kc = k_src[lo:lo + nk, :]
        s_scr[slot, :, :2 * nk] = lax.dot_general(q_src[j * blk:(j + 1) * blk, :],
                                                  per_head(kc, jnp.zeros_like(kc)), _NT,
                                                  preferred_element_type=F32)

    def softmax(b, slot):
        gi = b[0]
        win, dil = DIL_GROUPS[gi]
        nk = geometry(*b)[2]
        rt = blk // 2
        qi = lax.broadcasted_iota(jnp.int32, (rt, nk), 0)
        kj = lax.broadcasted_iota(jnp.int32, (rt, nk), 1)
        for t in range(blk // rt):
            diff = (nk - blk) + (qi + t * rt) - kj
            valid = (diff >= 0) & (diff <= win // dil)
            ms = []
            for h in range(HEADS_PER_GROUP):
                s = jnp.where(valid, s_scr[slot, t * rt:(t + 1) * rt, h * nk:(h + 1) * nk], NEG_INF)
                m = jnp.max(s, axis=-1, keepdims=True)
                p_scr[slot, t * rt:(t + 1) * rt, h * nk:(h + 1) * nk] = jnp.exp(s - m).astype(BF16)
                ms.append(m)
            m_scr[gi, token_rows(*b, t * rt, rt), :] = jnp.where(head0, ms[0], ms[1])

    def weighted_values(b, slot):
        gi = b[0]
        _, lo, nk, (_, _, v_src) = geometry(*b)
        vc = v_src[lo:lo + nk, :]
        zero = jnp.zeros_like(vc)
        v2 = jnp.concatenate([per_head(vc, zero), head_ones(nk)], axis=1)
        pvl = _dot(p_scr[slot, :, :2 * nk], v2)
        rows = token_rows(*b, 0, blk)
        acc_scr[gi, rows, :] = pvl[:, :LANES]
        l_scr[gi, rows, :] = pvl[:, LANES:]

    blocks = [(gi, rho, n) for gi, (_, dil) in enumerate(DIL_GROUPS)
              for rho in range(dil) for n in range(S // dil // blk)]
    nbat = ATTN_BATCH
    batches = [blocks[i:i + nbat] for i in range(0, len(blocks), nbat)]
    slot_of = lambda bi, i: (bi % 2) * nbat + i
    for i, b in enumerate(batches[0]):
        scores(b, slot_of(0, i))
    for bi, batch in enumerate(batches):
        if bi + 1 < len(batches):
            for i, b in enumerate(batches[bi + 1]):
                scores(b, slot_of(bi + 1, i))
        for i, b in enumerate(batch):
            softmax(b, slot_of(bi, i))
        for i, b in enumerate(batch):
            weighted_values(b, slot_of(bi, i))

    def merge(t, carry):
        rows = pl.ds(pl.multiple_of(t * blk, blk), blk)
        ms = [m_scr[g, rows, :] for g in range(len(DIL_GROUPS))]
        m_all = functools.reduce(jnp.maximum, ms)
        ws = [jnp.exp(m - m_all) for m in ms]
        num = functools.reduce(jnp.add, [w * acc_scr[g, rows, :] for g, w in enumerate(ws)])
        den = functools.reduce(jnp.add, [w * l_scr[g, rows, :] for g, w in enumerate(ws)])
        o_ref[rows, :] = (num / den).astype(o_ref.dtype)
        return carry

    lax.fori_loop(0, S // blk, merge, 0)


def _attn(q, k_sh, v_sh):
    B, S, D = k_sh.shape
    n_groups = len(DIL_GROUPS)
    n_pairs = D // LANES
    qspec = lambda g: pl.BlockSpec((None, S, LANES), lambda b, p: (b, 0, g * n_pairs + p))
    blk = pl.BlockSpec((None, S, LANES), lambda b, p: (b, 0, p))
    res = lambda: pltpu.VMEM((n_groups - 1, S, LANES), BF16)
    nat = lambda: pltpu.VMEM((n_groups, S, LANES), F32)
    return pl.pallas_call(
        _attn_kernel,
        grid=(B, n_pairs),
        in_specs=[qspec(0), qspec(1), qspec(2), blk, blk],
        out_specs=blk,
        out_shape=jax.ShapeDtypeStruct((B, S, D), BF16),
        scratch_shapes=[pltpu.VMEM((S, LANES), F32), res(), res(), res(), nat(), nat(), nat(),
                        pltpu.VMEM((2 * ATTN_BATCH, BAND_BLOCK, 2 * HEADS_PER_GROUP * BAND_BLOCK), F32),
                        pltpu.VMEM((2 * ATTN_BATCH, BAND_BLOCK, 2 * HEADS_PER_GROUP * BAND_BLOCK), BF16)],
        compiler_params=pltpu.CompilerParams(vmem_limit_bytes=VMEM_LIMIT),
        name="attn",
    )(q, q, q, k_sh, v_sh)


def _b_post_kernel(att_ref, sg_ref, x_ref, mod_ref, wout_ref, out_ref):
    D = x_ref.shape[-1]
    gate = mod_ref[...][:, 2 * D:]
    y = att_ref[...].astype(F32) * sg_ref[...].astype(F32)
    out_ref[...] = x_ref[...] + gate * _mm(y, wout_ref[...])


def _b_post(att, sg, xr, mod3, w_out):
    B, S, D = xr.shape
    ts = SEQ_TILE
    tile = pl.BlockSpec((None, ts, D), lambda b, i: (b, i, 0))
    return pl.pallas_call(
        _b_post_kernel,
        grid=(B, S // ts),
        in_specs=[tile, tile, tile, pl.BlockSpec((None, 1, 3 * D), lambda b, i: (b, 0, 0)),
                  pl.BlockSpec(w_out.shape, lambda b, i: (0, 0))],
        out_specs=tile,
        out_shape=jax.ShapeDtypeStruct((B, S, D), F32),
        compiler_params=pltpu.CompilerParams(vmem_limit_bytes=VMEM_LIMIT),
        name="b_post",
    )(att, sg, xr, mod3, w_out)


def _rope_tables(seq):
    pos = jnp.arange(seq, dtype=F32)
    inv = ROPE_THETA ** (-jnp.arange(0, HEAD_DIM, 2, dtype=F32) / HEAD_DIM)
    ang = pos[:, None] * inv[None, :]
    cos, sin = jnp.cos(ang), jnp.sin(ang)
    cos_t = jnp.concatenate([cos, cos] * HEADS_PER_GROUP, axis=-1)
    sin_t = jnp.concatenate([-sin, sin] * HEADS_PER_GROUP, axis=-1)
    return cos_t, sin_t


def kernel(x, c, a_ada_w, a_ada_b, a_norm_g, a_mix_mu, a_w_in, a_w0, a_w1, a_w2, a_a0, a_a1, a_a2,
           a_k_k, a_k_a, a_r_k, a_ln_g, a_ln_b, a_w_out, kv_norm_g, w_kv, k_norm_g,
           b_ada_w, b_ada_b, b_norm_g, b_w_in, b_q_norm_g, b_w_out):
    B, S, D = x.shape
    assert a_ada_w.shape[0] == 1 and b_ada_w.shape[0] == 1
    assert D % LANES == 0 and S % (BAND_BLOCK * DIL_GROUPS[-1][1]) == 0 and S % SEQ_TILE == 0
    assert all(win // dil == BAND_BLOCK for win, dil in DIL_GROUPS)
    assert b_w_in.shape[-1] == (len(DIL_GROUPS) + 1) * D
    row = lambda t: t.reshape(1, -1)
    per_head = lambda t: jnp.tile(t.reshape(1, HEAD_DIM), (1, D // HEAD_DIM))

    mod_a, mod_b = _adaln(c, a_ada_w, a_ada_b, b_ada_w, b_ada_b)
    mod_a = mod_a.reshape(B, 1, 3 * D)
    mod_b = mod_b.reshape(B, 1, 3 * D)
    cos_t, sin_t = _rope_tables(S)

    r, lw, k, v, an, bn, sg_a = _a_pre(
        x, mod_a, a_norm_g, a_mix_mu[0], a_w_in[0].astype(BF16), a_w0, a_w1[0].astype(BF16),
        a_w2[0].astype(BF16), a_a0, a_a1[0].astype(BF16), a_a2[0].astype(BF16), a_k_k, a_k_a)
    y = _wkv(r, lw, k, v, an, bn)
    xr, k_sh, v_sh = _a_post(
        y, r, k, v, sg_a, x, mod_a, a_ln_g, a_ln_b, row(a_r_k[0]), a_w_out[0].astype(BF16),
        row(kv_norm_g), w_kv.astype(BF16), per_head(k_norm_g), cos_t, sin_t)

    q, sg_b = _b_pre(xr, mod_b, b_norm_g, b_w_in[0].astype(BF16), per_head(b_q_norm_g[0]), cos_t, sin_t)
    att = _attn(q, k_sh, v_sh)
    return _b_post(att, sg_b, xr, mod_b, b_w_out[0].astype(BF16))
```

```python
import functools

import jax
import jax.numpy as jnp
from jax import lax
from jax.experimental import pallas as pl
from jax.experimental.pallas import tpu as pltpu

F32 = jnp.float32
BF16 = jnp.bfloat16

HEAD_DIM = 64
LANES = 128
HEADS_PER_GROUP = LANES // HEAD_DIM
DIL_GROUPS = ((128, 1), (512, 4), (2048, 16))
BAND_BLOCK = 128
ROPE_THETA = 10000.0
NORM_EPS = 1e-6
GN_EPS = 64e-5
NEG_INF = -1e30
DECAY_SCALE = 0.6065306597126334
CHUNK = 64
WKV_PHASE_GAP = 4
WKV_SCAN_GAP = 2
ATTN_BATCH = 4
SEQ_TILE = 256
VMEM_LIMIT = 56 * 1024 * 1024

_NT = (((1,), (1,)), ((), ()))
_TN = (((0,), (0,)), ((), ()))


def _dot(a, b):
    return jnp.dot(a, b, preferred_element_type=F32)


def _mm(a, b):
    return _dot(a.astype(BF16), b.astype(BF16))


def _split_dot_lhs(e, x):
    hi = x.astype(BF16)
    lo = (x - hi.astype(F32)).astype(BF16)
    return _dot(e, hi) + _dot(e, lo)


def _mm_3pass(a, b):
    a_hi = a.astype(BF16)
    a_lo = (a - a_hi.astype(F32)).astype(BF16)
    b_hi = b.astype(BF16)
    b_lo = (b - b_hi.astype(F32)).astype(BF16)
    return _dot(a_hi, b_hi) + (_dot(a_hi, b_lo) + _dot(a_lo, b_hi))


def _head_sum(x, e):
    parts = [_dot(x[:, g * LANES:(g + 1) * LANES].astype(BF16), e) for g in range(x.shape[1] // LANES)]
    return parts[0] if len(parts) == 1 else jnp.concatenate(parts, axis=1)


def _rot_half(x):
    lane = lax.broadcasted_iota(jnp.int32, (1, LANES), 1)
    first = (lane % HEAD_DIM) < (HEAD_DIM // 2)
    parts = []
    for g in range(x.shape[1] // LANES):
        xg = x[:, g * LANES:(g + 1) * LANES]
        up = pltpu.roll(xg, LANES - HEAD_DIM // 2, axis=1)
        dn = pltpu.roll(xg, HEAD_DIM // 2, axis=1)
        parts.append(jnp.where(first, up, dn))
    return parts[0] if len(parts) == 1 else jnp.concatenate(parts, axis=1)


def _tile_lanes(t, n):
    return t if n == 1 else jnp.concatenate([t] * n, axis=1)


def _rms_rows(x):
    return x * lax.rsqrt(jnp.mean(x * x, axis=-1, keepdims=True) + NORM_EPS)


def _sigmoid(x):
    return 1.0 / (1.0 + jnp.exp(-x))


def _head_ones():
    r = lax.broadcasted_iota(jnp.int32, (LANES, LANES), 0) // HEAD_DIM
    c = lax.broadcasted_iota(jnp.int32, (LANES, LANES), 1) // HEAD_DIM
    return (r == c).astype(BF16)


def _adaln_kernel(c_ref, wa_ref, ba_ref, wb_ref, bb_ref, oa_ref, ob_ref):
    c = c_ref[...]
    sc = c * _sigmoid(c)
    oa_ref[...] = _mm_3pass(sc, wa_ref[...]) + ba_ref[...]
    ob_ref[...] = _mm_3pass(sc, wb_ref[...]) + bb_ref[...]


def _adaln(c, wa, ba, wb, bb):
    B, D = c.shape
    n3 = wa.shape[-1]
    tn = 512
    wspec = pl.BlockSpec((None, D, tn), lambda j: (0, 0, j))
    bspec = pl.BlockSpec((1, tn), lambda j: (0, j))
    ospec = pl.BlockSpec((B, tn), lambda j: (0, j))
    return pl.pallas_call(
        _adaln_kernel,
        grid=(n3 // tn,),
        in_specs=[pl.BlockSpec((B, D), lambda j: (0, 0)), wspec, bspec, wspec, bspec],
        out_specs=[ospec, ospec],
        out_shape=[jax.ShapeDtypeStruct((B, n3), F32)] * 2,
        name="adaln",
    )(c, wa, ba, wb, bb)


def _a_pre_kernel(x_ref, xp_ref, mod_ref, ng_ref, mu_ref, win_ref, w0_ref, w1_ref, w2_ref,
                  a0_ref, a1_ref, a2_ref, kk_ref, ka_ref,
                  r_out, lw_out, k_out, v_out, an_out, bn_out, sg_out):
    D = x_ref.shape[-1]
    i = pl.program_id(1)
    mod = mod_ref[...]
    shift, scale = mod[:, :D], mod[:, D:2 * D]
    gain = ng_ref[...] * (1.0 + scale)

    h = _rms_rows(x_ref[...]) * gain + shift
    hp = _rms_rows(xp_ref[...]) * gain + shift
    hp = jnp.where(i == 0, 0.0, hp[7:8, :])
    row = lax.broadcasted_iota(jnp.int32, h.shape, 0)
    hs = jnp.where(row == 0, hp, pltpu.roll(h, 1, axis=0))
    xx = hs - h
    mu = mu_ref[...]
    xs = [(h + xx * mu[p:p + 1, :]).astype(BF16) for p in range(6)]

    r = _dot(xs[0], win_ref[:, 0 * D:1 * D])
    k = _dot(xs[1], win_ref[:, 1 * D:2 * D])
    v = _dot(xs[2], win_ref[:, 2 * D:3 * D])
    g = _dot(xs[3], win_ref[:, 3 * D:4 * D])
    wl = _mm(jnp.tanh(_mm(xs[4], w1_ref[...])), w2_ref[...])
    al = _mm(_mm(xs[5], a1_ref[...]), a2_ref[...])

    a = _sigmoid(a0_ref[...] + al)

    kk = k * kk_ref[...]
    ss = _head_sum(kk * kk, _head_ones())
    kk = kk * lax.rsqrt(jnp.maximum(ss, 1e-24))

    r_out[...] = r.astype(r_out.dtype)
    lw_out[...] = (-DECAY_SCALE) * _sigmoid(w0_ref[...] + wl)
    k_out[...] = (k * (1.0 + (a - 1.0) * ka_ref[...])).astype(k_out.dtype)
    v_out[...] = v.astype(v_out.dtype)
    an_out[...] = (-kk).astype(an_out.dtype)
    bn_out[...] = (kk * a).astype(bn_out.dtype)
    sg_out[...] = (g * _sigmoid(g)).astype(sg_out.dtype)


def _a_pre(x, mod3, ng, mu, w_in, w0, w1, w2, a0, a1, a2, k_k, k_a):
    B, S, D = x.shape
    ts = SEQ_TILE
    tile = pl.BlockSpec((None, ts, D), lambda b, i: (b, i, 0))
    prev = pl.BlockSpec((None, 8, D), lambda b, i: (b, jnp.maximum(i * (ts // 8) - 1, 0), 0))
    vec = pl.BlockSpec((1, D), lambda b, i: (0, 0))

    def full(a):
        return pl.BlockSpec(a.shape, lambda b, i: (0,) * a.ndim)

    act = lambda dt: jax.ShapeDtypeStruct((B, S, D), dt)
    return pl.pallas_call(
        _a_pre_kernel,
        grid=(B, S // ts),
        in_specs=[tile, prev, pl.BlockSpec((None, 1, 3 * D), lambda b, i: (b, 0, 0)), vec, full(mu),
                  full(w_in), vec, full(w1), full(w2), vec, full(a1), full(a2), vec, vec],
        out_specs=[tile] * 7,
        out_shape=[act(BF16), act(F32), act(BF16), act(BF16), act(BF16), act(BF16), act(BF16)],
        compiler_params=pltpu.CompilerParams(vmem_limit_bytes=VMEM_LIMIT),
        name="a_pre",
    )(x, x, mod3, ng, mu, w_in, w0, w1, w2, a0, a1, a2, k_k, k_a)


def _wkv_kernel(r_ref, lw_ref, k_ref, v_ref, a_ref, b_ref, y_ref,
                lp_scr, t_scr, aak_scr, ark_scr, arb_scr, ats_scr, vs_scr, rt_scr, kend_scr, bend_scr,
                wend_scr, rh_scr, y0_scr, p_scr, q_scr, st_scr):
    C = CHUNK
    n_chunks = r_ref.shape[0] // C
    assert C == HEAD_DIM
    step = pl.program_id(0)
    cur = step % 2
    prev = 1 - cur

    lane = lax.broadcasted_iota(jnp.int32, (1, LANES), 1)
    head0 = lane < HEAD_DIM
    ti = lax.broadcasted_iota(jnp.int32, (C, LANES), 0)
    ii = lax.broadcasted_iota(jnp.int32, (C, LANES), 1) % C
    strict = ii < ti
    incl = ii <= ti
    eye_pair = (ii == ti).astype(F32)
    ri = lax.broadcasted_iota(jnp.int32, (LANES, LANES), 0)
    ci = lax.broadcasted_iota(jnp.int32, (LANES, LANES), 1)
    same = (ri // HEAD_DIM) == (ci // HEAD_DIM)
    eye = (ri == ci).astype(F32)
    tri = (lax.broadcasted_iota(jnp.int32, (C, C), 1)
           <= lax.broadcasted_iota(jnp.int32, (C, C), 0)).astype(BF16)

    def stack(x):
        zero = jnp.zeros_like(x)
        return jnp.concatenate([jnp.where(head0, x, zero), jnp.where(head0, zero, x)], axis=0)

    def chunk_rows(c):
        return slice(c * C, (c + 1) * C)

    def gram_terms(c):
        rows = chunk_rows(c)
        cum = _split_dot_lhs(tri, lw_ref[rows, :])
        yield
        lw = lw_ref[rows, :]
        r, k, v, a, b = (ref[rows, :].astype(F32) for ref in (r_ref, k_ref, v_ref, a_ref, b_ref))
        e_cum = jnp.exp(cum)
        e_inv = jnp.exp(-cum)
        e_end = jnp.exp(cum[C - 1:C, :] - cum)
        rt = r * e_cum
        at = (a * jnp.exp(cum - lw)).astype(BF16)
        kt = (k * e_inv).astype(BF16)
        bt = (b * e_inv).astype(BF16)
        kb = jnp.concatenate([stack(kt), stack(bt)], axis=0)
        ga = lax.dot_general(at, kb, _NT, preferred_element_type=F32)
        gr = lax.dot_general(rt.astype(BF16), kb, _NT, preferred_element_type=F32)
        ats_scr[c] = stack(at)
        vs_scr[c] = stack(v.astype(BF16))
        rt_scr[rows, :] = rt
        kend_scr[rows, :] = (k * e_end).astype(BF16)
        bend_scr[rows, :] = (b * e_end).astype(BF16)
        wend_scr[c] = jnp.broadcast_to(e_cum[C - 1:C, :], (8, LANES))
        yield
        a_ab = jnp.where(strict, ga[:, LANES:], 0.0)
        lp_scr[c] = a_ab.astype(BF16)
        t_scr[c] = eye_pair + a_ab
        aak_scr[c] = jnp.where(strict, ga[:, :LANES], 0.0).astype(BF16)
        ark_scr[c] = jnp.where(incl, gr[:, :LANES], 0.0).astype(BF16)
        arb_scr[c] = jnp.where(incl, gr[:, LANES:], 0.0).astype(BF16)

    def square_only(c):
        lp = lp_scr[c]
        lp2 = _dot(lp, stack(lp))
        yield
        lp_scr[c] = lp2.astype(BF16)

    def fold_and_square(c):
        lp = lp_scr[c]
        lp_bd = stack(lp)
        t = t_scr[c]
        tl = _dot(t.astype(BF16), lp_bd)
        lp2 = _dot(lp, lp_bd)
        yield
        t_scr[c] = t + tl
        lp_scr[c] = lp2.astype(BF16)

    def fold_only(c):
        t = t_scr[c]
        tl = _dot(t.astype(BF16), stack(lp_scr[c]))
        yield
        t_scr[c] = t + tl

    def chunk_maps(c):
        rows = chunk_rows(c)
        vs = vs_scr[c]
        x = _dot(aak_scr[c], vs)
        y0a = _dot(ark_scr[c], vs)
        yield
        au = _dot(t_scr[c].astype(BF16),
                  jnp.concatenate([ats_scr[c], stack(x.astype(BF16))], axis=1))
        yield
        ah = au[:, :LANES].astype(BF16)
        u0 = au[:, LANES:].astype(BF16)
        ry = _dot(arb_scr[c], jnp.concatenate([stack(ah), stack(u0)], axis=1))
        k_end = kend_scr[rows, :]
        b_end = bend_scr[rows, :]
        p = lax.dot_general(b_end, ah, _TN, preferred_element_type=F32)
        q = lax.dot_general(jnp.concatenate([k_end, b_end], axis=0),
                            jnp.concatenate([v_ref[rows, :], u0], axis=0), _TN,
                            preferred_element_type=F32)
        yield
        rh_scr[cur, rows, :] = rt_scr[rows, :] + ry[:, :LANES]
        y0_scr[cur, rows, :] = y0a + ry[:, LANES:]
        p_scr[cur, c] = jnp.where(same, p, 0.0) + eye * wend_scr[c][0:1, :]
        q_scr[cur, c] = jnp.where(same, q, 0.0)

    def scan_chunks():
        for c in range(n_chunks):
            rows = chunk_rows(c)
            st = st_scr[...]
            ys = _mm(rh_scr[prev, rows, :], st)
            st_new = _mm_3pass(p_scr[prev, c], st)
            yield
            y_ref[rows, :] = (y0_scr[prev, rows, :] + ys).astype(y_ref.dtype)
            st_scr[...] = st_new + q_scr[prev, c]

    stages = [gram_terms, square_only]
    n = 4
    while n < C:
        stages.append(fold_and_square)
        n *= 2
    stages += [fold_only, chunk_maps]

    @pl.when(step == 0)
    def _():
        rh_scr[1] = jnp.zeros(rh_scr.shape[1:], F32)
        y0_scr[1] = jnp.zeros(y0_scr.shape[1:], F32)
        p_scr[1] = jnp.zeros(p_scr.shape[1:], F32)
        q_scr[1] = jnp.zeros(q_scr.shape[1:], F32)

    def chunk_pipeline(c):
        for stage in stages:
            yield from stage(c)

    st_scr[...] = jnp.zeros((LANES, LANES), F32)
    gens = [chunk_pipeline(c) for c in range(n_chunks)]
    scan = scan_chunks()
    live = set(range(n_chunks))
    t = 0
    scan_live = True
    while live or scan_live:
        for i in sorted(live):
            if t >= i and (t - i) % WKV_PHASE_GAP == 0:
                try:
                    next(gens[i])
                except StopIteration:
                    live.discard(i)
        if scan_live and t % WKV_SCAN_GAP == 0:
            try:
                next(scan)
            except StopIteration:
                scan_live = False
        t += 1


def _wkv(r, lw, k, v, an, bn):
    B, S, D = r.shape
    n_pairs = D // LANES
    n_steps = B * n_pairs
    n_chunks = S // CHUNK
    in_blk = pl.BlockSpec((None, S, LANES), lambda s: (jnp.minimum(s, n_steps - 1) // n_pairs, 0,
                                                       jnp.minimum(s, n_steps - 1) % n_pairs))
    out_blk = pl.BlockSpec((None, S, LANES), lambda s: (jnp.maximum(s - 1, 0) // n_pairs, 0,
                                                        jnp.maximum(s - 1, 0) % n_pairs))
    mat = lambda dt: pltpu.VMEM((n_chunks, LANES, LANES), dt)
    pair = lambda dt: pltpu.VMEM((n_chunks, CHUNK, LANES), dt)
    seq = lambda dt: pltpu.VMEM((S, LANES), dt)
    return pl.pallas_call(
        _wkv_kernel,
        grid=(n_steps + 1,),
        in_specs=[in_blk] * 6,
        out_specs=out_blk,
        out_shape=jax.ShapeDtypeStruct((B, S, D), BF16),
        scratch_shapes=[pair(BF16), pair(F32), pair(BF16), pair(BF16), pair(BF16), mat(BF16), mat(BF16),
                        seq(F32), seq(BF16), seq(BF16), pltpu.VMEM((n_chunks, 8, LANES), F32),
                        pltpu.VMEM((2, S, LANES), F32), pltpu.VMEM((2, S, LANES), F32),
                        pltpu.VMEM((2, n_chunks, LANES, LANES), F32),
                        pltpu.VMEM((2, n_chunks, LANES, LANES), F32),
                        pltpu.VMEM((LANES, LANES), F32)],
        compiler_params=pltpu.CompilerParams(vmem_limit_bytes=VMEM_LIMIT,
                                             dimension_semantics=("arbitrary",)),
        name="wkv",
    )(r, lw, k, v, an, bn)


def _a_post_kernel(y_ref, r_ref, k_ref, v_ref, sg_ref, x_ref, mod_ref, lng_ref, lnb_ref, rk_ref,
                   wout_ref, kvg_ref, wkv_ref, kng_ref, cos_ref, sin_ref,
                   xr_out, ksh_out, vsh_out):
    D = x_ref.shape[-1]
    e = _head_ones()
    n_lane_groups = D // LANES
    f32 = lambda ref: ref[...].astype(F32)
    y = f32(y_ref)
    mean = _head_sum(y, e) * (1.0 / HEAD_DIM)
    d = y - mean
    var = _head_sum(d * d, e) * (1.0 / HEAD_DIM)
    yn = d * lax.rsqrt(var + GN_EPS) * lng_ref[...] + lnb_ref[...]
    bonus = _head_sum(f32(r_ref) * f32(k_ref) * rk_ref[...], e) * f32(v_ref)
    mix = _mm((yn + bonus) * f32(sg_ref), wout_ref[...])
    gate = mod_ref[...][:, 2 * D:]
    xr = x_ref[...] + gate * mix
    xr_out[...] = xr

    kv = _mm(_rms_rows(xr) * kvg_ref[...], wkv_ref[...])
    ks = kv[:, :D]
    ks = ks * lax.rsqrt(_head_sum(ks * ks, e) * (1.0 / HEAD_DIM) + NORM_EPS) * kng_ref[...]
    cos = _tile_lanes(cos_ref[...], n_lane_groups)
    sin = _tile_lanes(sin_ref[...], n_lane_groups)
    ksh_out[...] = (ks * cos + _rot_half(ks) * sin).astype(ksh_out.dtype)
    vsh_out[...] = kv[:, D:].astype(vsh_out.dtype)


def _a_post(y, r, k, v, sg, x, mod3, ln_g, ln_b, r_k, w_out, kv_g, w_kv, kn_g, cos_t, sin_t):
    B, S, D = x.shape
    ts = SEQ_TILE
    tile = pl.BlockSpec((None, ts, D), lambda b, i: (b, i, 0))
    vec = pl.BlockSpec((1, D), lambda b, i: (0, 0))
    rope = pl.BlockSpec((ts, LANES), lambda b, i: (i, 0))

    def full(a):
        return pl.BlockSpec(a.shape, lambda b, i: (0,) * a.ndim)

    act = lambda dt: jax.ShapeDtypeStruct((B, S, D), dt)
    return pl.pallas_call(
        _a_post_kernel,
        grid=(B, S // ts),
        in_specs=[tile] * 6 + [pl.BlockSpec((None, 1, 3 * D), lambda b, i: (b, 0, 0)), vec, vec, vec,
                               full(w_out), vec, full(w_kv), vec, rope, rope],
        out_specs=[tile] * 3,
        out_shape=[act(F32), act(BF16), act(BF16)],
        compiler_params=pltpu.CompilerParams(vmem_limit_bytes=VMEM_LIMIT),
        name="a_post",
    )(y, r, k, v, sg, x, mod3, ln_g, ln_b, r_k, w_out, kv_g, w_kv, kn_g, cos_t, sin_t)


def _b_pre_kernel(x_ref, mod_ref, ng_ref, win_ref, qg_ref, cos_ref, sin_ref, q_out, sg_out):
    D = x_ref.shape[-1]
    nq = q_out.shape[-1]
    e = _head_ones()
    mod = mod_ref[...]
    shift, scale = mod[:, :D], mod[:, D:2 * D]
    h = (_rms_rows(x_ref[...]) * (ng_ref[...] * (1.0 + scale)) + shift).astype(BF16)
    cos = _tile_lanes(cos_ref[...], D // LANES)
    sin = _tile_lanes(sin_ref[...], D // LANES)
    for g in range(nq // D):
        q = _dot(h, win_ref[:, g * D:(g + 1) * D])
        q = q * lax.rsqrt(_head_sum(q * q, e) * (1.0 / HEAD_DIM) + NORM_EPS) * qg_ref[...]
        q_out[:, g * D:(g + 1) * D] = ((q * cos + _rot_half(q) * sin) * (HEAD_DIM ** -0.5)).astype(q_out.dtype)
    gate = _dot(h, win_ref[:, nq:])
    sg_out[...] = (gate * _sigmoid(gate)).astype(sg_out.dtype)


def _b_pre(xr, mod3, ng, w_in, qn_g, cos_t, sin_t):
    B, S, D = xr.shape
    nq = w_in.shape[1] - D
    ts = SEQ_TILE
    tile = pl.BlockSpec((None, ts, D), lambda b, i: (b, i, 0))
    vec = pl.BlockSpec((1, D), lambda b, i: (0, 0))
    rope = pl.BlockSpec((ts, LANES), lambda b, i: (i, 0))
    return pl.pallas_call(
        _b_pre_kernel,
        grid=(B, S // ts),
        in_specs=[tile, pl.BlockSpec((None, 1, 3 * D), lambda b, i: (b, 0, 0)), vec,
                  pl.BlockSpec(w_in.shape, lambda b, i: (0, 0)), vec, rope, rope],
        out_specs=[pl.BlockSpec((None, ts, nq), lambda b, i: (b, i, 0)), tile],
        out_shape=[jax.ShapeDtypeStruct((B, S, nq), BF16), jax.ShapeDtypeStruct((B, S, D), BF16)],
        compiler_params=pltpu.CompilerParams(vmem_limit_bytes=VMEM_LIMIT),
        name="b_pre",
    )(xr, mod3, ng, w_in, qn_g, cos_t, sin_t)


def _attn_kernel(q0_ref, q1_ref, q2_ref, k_ref, v_ref, o_ref,
                 tmp_scr, qr_scr, kr_scr, vr_scr, acc_scr, m_scr, l_scr, s_scr, p_scr):
    S = k_ref.shape[0]
    blk = BAND_BLOCK
    lane = lax.broadcasted_iota(jnp.int32, (1, LANES), 1)
    head0 = lane < HEAD_DIM

    def head_ones(nk):
        hsel = (lax.broadcasted_iota(jnp.int32, (HEADS_PER_GROUP * nk, LANES), 1) // HEAD_DIM
                == lax.broadcasted_iota(jnp.int32, (HEADS_PER_GROUP * nk, LANES), 0) // nk)
        return hsel.astype(F32).astype(BF16)

    def upcast(src_ref):
        def tile(t, carry):
            rows = pl.ds(pl.multiple_of(t * blk, blk), blk)
            tmp_scr[rows, :] = src_ref[rows, :].astype(F32)
            return carry
        lax.fori_loop(0, S // blk, tile, 0)

    for gi, (_, dil) in enumerate(DIL_GROUPS):
        if dil == 1:
            continue
        seg = S // dil
        for src_ref, dst_scr in ((q1_ref if gi == 1 else q2_ref, qr_scr), (k_ref, kr_scr), (v_ref, vr_scr)):
            upcast(src_ref)
            for rho in range(dil):
                dst_scr[gi - 1, rho * seg:(rho + 1) * seg, :] = (
                    tmp_scr[pl.ds(rho, seg, stride=dil), :].astype(BF16))

    def geometry(gi, rho, n):
        dil = DIL_GROUPS[gi][1]
        j = rho * (S // dil // blk) + n
        lo = (j - 1) * blk if n > 0 else j * blk
        nk = (j + 1) * blk - lo
        if gi == 0:
            srcs = (q0_ref, k_ref, v_ref)
        else:
            srcs = (qr_scr.at[gi - 1], kr_scr.at[gi - 1], vr_scr.at[gi - 1])
        return j, lo, nk, srcs

    def token_rows(gi, rho, n, start, size):
        dil = DIL_GROUPS[gi][1]
        first = rho + dil * (blk * n + start)
        return slice(first, first + size) if dil == 1 else pl.ds(first, size, stride=dil)

    def per_head(x, other):
        return jnp.concatenate([jnp.where(head0, x, other), jnp.where(head0, other, x)], axis=0)

    def scores(b, slot):
        j, lo, nk, (q_src, k_src, _) = geometry(*b)
        kc = k_src[lo:lo + nk, :]
        s_scr[slot, :, :2 * nk] = lax.dot_general(q_src[j * blk:(j + 1) * blk, :],
                                                  per_head(kc, jnp.zeros_like(kc)), _NT,
                                                  preferred_element_type=F32)

    def softmax(b, slot):
        gi = b[0]
        win, dil = DIL_GROUPS[gi]
        nk = geometry(*b)[2]
        rt = blk // 2
        qi = lax.broadcasted_iota(jnp.int32, (rt, nk), 0)
        kj = lax.broadcasted_iota(jnp.int32, (rt, nk), 1)
        for t in range(blk // rt):
            diff = (nk - blk) + (qi + t * rt) - kj
            valid = (diff >= 0) & (diff <= win // dil)
            ms = []
            for h in range(HEADS_PER_GROUP):
                s = jnp.where(valid, s_scr[slot, t * rt:(t + 1) * rt, h * nk:(h + 1) * nk], NEG_INF)
                m = jnp.max(s, axis=-1, keepdims=True)
                p_scr[slot, t * rt:(t + 1) * rt, h * nk:(h + 1) * nk] = jnp.exp(s - m).astype(BF16)
                ms.append(m)
            m_scr[gi, token_rows(*b, t * rt, rt), :] = jnp.where(head0, ms[0], ms[1])

    def weighted_values(b, slot):
        gi = b[0]
        _, lo, nk, (_, _, v_src) = geometry(*b)
        vc = v_src[lo:lo + nk, :]
        zero = jnp.zeros_like(vc)
        v2 = jnp.concatenate([per_head(vc, zero), head_ones(nk)], axis=1)
        pvl = _dot(p_scr[slot, :, :2 * nk], v2)
        rows = token_rows(*b, 0, blk)
        acc_scr[gi, rows, :] = pvl[:, :LANES]
        l_scr[gi, rows, :] = pvl[:, LANES:]

    blocks = [(gi, rho, n) for gi, (_, dil) in enumerate(DIL_GROUPS)
              for rho in range(dil) for n in range(S // dil // blk)]
    nbat = ATTN_BATCH
    batches = [blocks[i:i + nbat] for i in range(0, len(blocks), nbat)]
    slot_of = lambda bi, i: (bi % 2) * nbat + i
    for i, b in enumerate(batches[0]):
        scores(b, slot_of(0, i))
    for bi, batch in enumerate(batches):
        if bi + 1 < len(batches):
            for i, b in enumerate(batches[bi + 1]):
                scores(b, slot_of(bi + 1, i))
        for i, b in enumerate(batch):
            softmax(b, slot_of(bi, i))
        for i, b in enumerate(batch):
            weighted_values(b, slot_of(bi, i))

    def merge(t, carry):
        rows = pl.ds(pl.multiple_of(t * blk, blk), blk)
        ms = [m_scr[g, rows, :] for g in range(len(DIL_GROUPS))]
        m_all = functools.reduce(jnp.maximum, ms)
        ws = [jnp.exp(m - m_all) for m in ms]
        num = functools.reduce(jnp.add, [w * acc_scr[g, rows, :] for g, w in enumerate(ws)])
        den = functools.reduce(jnp.add, [w * l_scr[g, rows, :] for g, w in enumerate(ws)])
        o_ref[rows, :] = (num / den).astype(o_ref.dtype)
        return carry

    lax.fori_loop(0, S // blk, merge, 0)


def _attn(q, k_sh, v_sh):
    B, S, D = k_sh.shape
    n_groups = len(DIL_GROUPS)
    n_pairs = D // LANES
    qspec = lambda g: pl.BlockSpec((None, S, LANES), lambda b, p: (b, 0, g * n_pairs + p))
    blk = pl.BlockSpec((None, S, LANES), lambda b, p: (b, 0, p))
    res = lambda: pltpu.VMEM((n_groups - 1, S, LANES), BF16)
    nat = lambda: pltpu.VMEM((n_groups, S, LANES), F32)
    return pl.pallas_call(
        _attn_kernel,
        grid=(B, n_pairs),
        in_specs=[qspec(0), qspec(1), qspec(2), blk, blk],
        out_specs=blk,
        out_shape=jax.ShapeDtypeStruct((B, S, D), BF16),
        scratch_shapes=[pltpu.VMEM((S, LANES), F32), res(), res(), res(), nat(), nat(), nat(),
                        pltpu.VMEM((2 * ATTN_BATCH, BAND_BLOCK, 2 * HEADS_PER_GROUP * BAND_BLOCK), F32),
                        pltpu.VMEM((2 * ATTN_BATCH, BAND_BLOCK, 2 * HEADS_PER_GROUP * BAND_BLOCK), BF16)],
        compiler_params=pltpu.CompilerParams(vmem_limit_bytes=VMEM_LIMIT),
        name="attn",
    )(q, q, q, k_sh, v_sh)


def _b_post_kernel(att_ref, sg_ref, x_ref, mod_ref, wout_ref, out_ref):
    D = x_ref.shape[-1]
    gate = mod_ref[...][:, 2 * D:]
    y = att_ref[...].astype(F32) * sg_ref[...].astype(F32)
    out_ref[...] = x_ref[...] + gate * _mm(y, wout_ref[...])


def _b_post(att, sg, xr, mod3, w_out):
    B, S, D = xr.shape
    ts = SEQ_TILE
    tile = pl.BlockSpec((None, ts, D), lambda b, i: (b, i, 0))
    return pl.pallas_call(
        _b_post_kernel,
        grid=(B, S // ts),
        in_specs=[tile, tile, tile, pl.BlockSpec((None, 1, 3 * D), lambda b, i: (b, 0, 0)),
                  pl.BlockSpec(w_out.shape, lambda b, i: (0, 0))],
        out_specs=tile,
        out_shape=jax.ShapeDtypeStruct((B, S, D), F32),
        compiler_params=pltpu.CompilerParams(vmem_limit_bytes=VMEM_LIMIT),
        name="b_post",
    )(att, sg, xr, mod3, w_out)


def _rope_tables(seq):
    pos = jnp.arange(seq, dtype=F32)
    inv = ROPE_THETA ** (-jnp.arange(0, HEAD_DIM, 2, dtype=F32) / HEAD_DIM)
    ang = pos[:, None] * inv[None, :]
    cos, sin = jnp.cos(ang), jnp.sin(ang)
    cos_t = jnp.concatenate([cos, cos] * HEADS_PER_GROUP, axis=-1)
    sin_t = jnp.concatenate([-sin, sin] * HEADS_PER_GROUP, axis=-1)
    return cos_t, sin_t


def kernel(x, c, a_ada_w, a_ada_b, a_norm_g, a_mix_mu, a_w_in, a_w0, a_w1, a_w2, a_a0, a_a1, a_a2,
           a_k_k, a_k_a, a_r_k, a_ln_g, a_ln_b, a_w_out, kv_norm_g, w_kv, k_norm_g,
           b_ada_w, b_ada_b, b_norm_g, b_w_in, b_q_norm_g, b_w_out):
    B, S, D = x.shape
    assert a_ada_w.shape[0] == 1 and b_ada_w.shape[0] == 1
    assert D % LANES == 0 and S % (BAND_BLOCK * DIL_GROUPS[-1][1]) == 0 and S % SEQ_TILE == 0
    assert all(win // dil == BAND_BLOCK for win, dil in DIL_GROUPS)
    assert b_w_in.shape[-1] == (len(DIL_GROUPS) + 1) * D
    row = lambda t: t.reshape(1, -1)
    per_head = lambda t: jnp.tile(t.reshape(1, HEAD_DIM), (1, D // HEAD_DIM))

    mod_a, mod_b = _adaln(c, a_ada_w, a_ada_b, b_ada_w, b_ada_b)
    mod_a = mod_a.reshape(B, 1, 3 * D)
    mod_b = mod_b.reshape(B, 1, 3 * D)
    cos_t, sin_t = _rope_tables(S)

    r, lw, k, v, an, bn, sg_a = _a_pre(
        x, mod_a, a_norm_g, a_mix_mu[0], a_w_in[0].astype(BF16), a_w0, a_w1[0].astype(BF16),
        a_w2[0].astype(BF16), a_a0, a_a1[0].astype(BF16), a_a2[0].astype(BF16), a_k_k, a_k_a)
    y = _wkv(r, lw, k, v, an, bn)
    xr, k_sh, v_sh = _a_post(
        y, r, k, v, sg_a, x, mod_a, a_ln_g, a_ln_b, row(a_r_k[0]), a_w_out[0].astype(BF16),
        row(kv_norm_g), w_kv.astype(BF16), per_head(k_norm_g), cos_t, sin_t)

    q, sg_b = _b_pre(xr, mod_b, b_norm_g, b_w_in[0].astype(BF16), per_head(b_q_norm_g[0]), cos_t, sin_t)
    att = _attn(q, k_sh, v_sh)
    return _b_post(att, sg_b, xr, mod_b, b_w_out[0].astype(BF16))
```

```python
import functools

import jax
import jax.numpy as jnp
from jax import lax
from jax.experimental import pallas as pl
from jax.experimental.pallas import tpu as pltpu

F32 = jnp.float32
BF16 = jnp.bfloat16

HEAD_DIM = 64
LANES = 128
HEADS_PER_GROUP = LANES // HEAD_DIM
DIL_GROUPS = ((128, 1), (512, 4), (2048, 16))
BAND_BLOCK = 128
ROPE_THETA = 10000.0
NORM_EPS = 1e-6
GN_EPS = 64e-5
NEG_INF = -1e30
DECAY_SCALE = 0.6065306597126334
CHUNK = 64
WKV_PHASE_GAP = 4
WKV_SCAN_GAP = 2
ATTN_PHASE_GAP = 2
ATTN_SLOTS = 8
SEQ_TILE = 256
WIDE_SEQ_TILE = 512
Q_SCALE = HEAD_DIM ** -0.5 * 1.4426950408889634
VMEM_LIMIT = 56 * 1024 * 1024

_NT = (((1,), (1,)), ((), ()))
_TN = (((0,), (0,)), ((), ()))


def _dot(a, b):
    return jnp.dot(a, b, preferred_element_type=F32)


def _mm(a, b):
    return _dot(a.astype(BF16), b.astype(BF16))


def _split_dot_lhs(e, x):
    hi = x.astype(BF16)
    lo = (x - hi.astype(F32)).astype(BF16)
    return _dot(e, hi) + _dot(e, lo)


def _mm_3pass(a, b):
    a_hi = a.astype(BF16)
    a_lo = (a - a_hi.astype(F32)).astype(BF16)
    b_hi = b.astype(BF16)
    b_lo = (b - b_hi.astype(F32)).astype(BF16)
    return _dot(a_hi, b_hi) + (_dot(a_hi, b_lo) + _dot(a_lo, b_hi))


def _head_sum(x, e):
    parts = [_dot(x[:, g * LANES:(g + 1) * LANES].astype(BF16), e) for g in range(x.shape[1] // LANES)]
    return parts[0] if len(parts) == 1 else jnp.concatenate(parts, axis=1)


def _rot_half(x):
    lane = lax.broadcasted_iota(jnp.int32, (1, LANES), 1)
    first = (lane % HEAD_DIM) < (HEAD_DIM // 2)
    parts = []
    for g in range(x.shape[1] // LANES):
        xg = x[:, g * LANES:(g + 1) * LANES]
        up = pltpu.roll(xg, LANES - HEAD_DIM // 2, axis=1)
        dn = pltpu.roll(xg, HEAD_DIM // 2, axis=1)
        parts.append(jnp.where(first, up, dn))
    return parts[0] if len(parts) == 1 else jnp.concatenate(parts, axis=1)


def _tile_lanes(t, n):
    return t if n == 1 else jnp.concatenate([t] * n, axis=1)


def _rms_rows(x):
    return x * lax.rsqrt(jnp.mean(x * x, axis=-1, keepdims=True) + NORM_EPS)


def _sigmoid(x):
    return 1.0 / (1.0 + jnp.exp(-x))


def _head_ones():
    r = lax.broadcasted_iota(jnp.int32, (LANES, LANES), 0) // HEAD_DIM
    c = lax.broadcasted_iota(jnp.int32, (LANES, LANES), 1) // HEAD_DIM
    return (r == c).astype(BF16)


def _adaln_kernel(c_ref, wa_ref, ba_ref, wb_ref, bb_ref, oa_ref, ob_ref):
    c = c_ref[...]
    sc = c * _sigmoid(c)
    oa_ref[...] = _mm_3pass(sc, wa_ref[...]) + ba_ref[...]
    ob_ref[...] = _mm_3pass(sc, wb_ref[...]) + bb_ref[...]


def _adaln(c, wa, ba, wb, bb):
    B, D = c.shape
    n3 = wa.shape[-1]
    tn = 512
    wspec = pl.BlockSpec((None, D, tn), lambda j: (0, 0, j))
    bspec = pl.BlockSpec((1, tn), lambda j: (0, j))
    ospec = pl.BlockSpec((B, tn), lambda j: (0, j))
    return pl.pallas_call(
        _adaln_kernel,
        grid=(n3 // tn,),
        in_specs=[pl.BlockSpec((B, D), lambda j: (0, 0)), wspec, bspec, wspec, bspec],
        out_specs=[ospec, ospec],
        out_shape=[jax.ShapeDtypeStruct((B, n3), F32)] * 2,
        name="adaln",
    )(c, wa, ba, wb, bb)


def _a_pre_kernel(x_ref, xp_ref, mod_ref, ng_ref, mu_ref, win_ref, w0_ref, w1_ref, w2_ref,
                  a0_ref, a1_ref, a2_ref, kk_ref, ka_ref,
                  r_out, lw_out, k_out, v_out, an_out, bn_out, sg_out):
    D = x_ref.shape[-1]
    i = pl.program_id(1)
    mod = mod_ref[...]
    shift, scale = mod[:, :D], mod[:, D:2 * D]
    gain = ng_ref[...] * (1.0 + scale)

    h = _rms_rows(x_ref[...]) * gain + shift
    hp = _rms_rows(xp_ref[...]) * gain + shift
    hp = jnp.where(i == 0, 0.0, hp[7:8, :])
    row = lax.broadcasted_iota(jnp.int32, h.shape, 0)
    hs = jnp.where(row == 0, hp, pltpu.roll(h, 1, axis=0))
    xx = hs - h
    mu = mu_ref[...]
    xs = [(h + xx * mu[p:p + 1, :]).astype(BF16) for p in range(6)]

    r = _dot(xs[0], win_ref[:, 0 * D:1 * D])
    k = _dot(xs[1], win_ref[:, 1 * D:2 * D])
    v = _dot(xs[2], win_ref[:, 2 * D:3 * D])
    g = _dot(xs[3], win_ref[:, 3 * D:4 * D])
    wl = _mm(jnp.tanh(_mm(xs[4], w1_ref[...])), w2_ref[...])
    al = _mm(_mm(xs[5], a1_ref[...]), a2_ref[...])

    a = _sigmoid(a0_ref[...] + al)

    kk = k * kk_ref[...]
    ss = _head_sum(kk * kk, _head_ones())
    kk = kk * lax.rsqrt(jnp.maximum(ss, 1e-24))

    r_out[...] = r.astype(r_out.dtype)
    lw_out[...] = (-DECAY_SCALE) * _sigmoid(w0_ref[...] + wl)
    k_out[...] = (k * (1.0 + (a - 1.0) * ka_ref[...])).astype(k_out.dtype)
    v_out[...] = v.astype(v_out.dtype)
    an_out[...] = (-kk).astype(an_out.dtype)
    bn_out[...] = (kk * a).astype(bn_out.dtype)
    sg_out[...] = (g * _sigmoid(g)).astype(sg_out.dtype)


def _a_pre(x, mod3, ng, mu, w_in, w0, w1, w2, a0, a1, a2, k_k, k_a):
    B, S, D = x.shape
    ts = SEQ_TILE
    tile = pl.BlockSpec((None, ts, D), lambda b, i: (b, i, 0))
    prev = pl.BlockSpec((None, 8, D), lambda b, i: (b, jnp.maximum(i * (ts // 8) - 1, 0), 0))
    vec = pl.BlockSpec((1, D), lambda b, i: (0, 0))

    def full(a):
        return pl.BlockSpec(a.shape, lambda b, i: (0,) * a.ndim)

    act = lambda dt: jax.ShapeDtypeStruct((B, S, D), dt)
    return pl.pallas_call(
        _a_pre_kernel,
        grid=(B, S // ts),
        in_specs=[tile, prev, pl.BlockSpec((None, 1, 3 * D), lambda b, i: (b, 0, 0)), vec, full(mu),
                  full(w_in), vec, full(w1), full(w2), vec, full(a1), full(a2), vec, vec],
        out_specs=[tile] * 7,
        out_shape=[act(BF16), act(F32), act(BF16), act(BF16), act(BF16), act(BF16), act(BF16)],
        compiler_params=pltpu.CompilerParams(vmem_limit_bytes=VMEM_LIMIT),
        name="a_pre",
    )(x, x, mod3, ng, mu, w_in, w0, w1, w2, a0, a1, a2, k_k, k_a)


def _wkv_kernel(r_ref, lw_ref, k_ref, v_ref, a_ref, b_ref, y_ref,
                lp_scr, t_scr, aak_scr, ark_scr, arb_scr, ats_scr, vs_scr, rt_scr, kend_scr, bend_scr,
                wend_scr, rh_scr, y0_scr, p_scr, q_scr, st_scr):
    C = CHUNK
    n_chunks = r_ref.shape[0] // C
    assert C == HEAD_DIM
    step = pl.program_id(0)
    cur = step % 2
    prev = 1 - cur

    lane = lax.broadcasted_iota(jnp.int32, (1, LANES), 1)
    head0 = lane < HEAD_DIM
    ti = lax.broadcasted_iota(jnp.int32, (C, LANES), 0)
    ii = lax.broadcasted_iota(jnp.int32, (C, LANES), 1) % C
    strict = ii < ti
    incl = ii <= ti
    eye_pair = (ii == ti).astype(F32)
    ri = lax.broadcasted_iota(jnp.int32, (LANES, LANES), 0)
    ci = lax.broadcasted_iota(jnp.int32, (LANES, LANES), 1)
    same = (ri // HEAD_DIM) == (ci // HEAD_DIM)
    eye = (ri == ci).astype(F32)
    tri = (lax.broadcasted_iota(jnp.int32, (C, C), 1)
           <= lax.broadcasted_iota(jnp.int32, (C, C), 0)).astype(BF16)

    def stack(x):
        zero = jnp.zeros_like(x)
        return jnp.concatenate([jnp.where(head0, x, zero), jnp.where(head0, zero, x)], axis=0)

    def chunk_rows(c):
        return slice(c * C, (c + 1) * C)

    def gram_terms(c):
        rows = chunk_rows(c)
        cum = _split_dot_lhs(tri, lw_ref[rows, :])
        yield
        lw = lw_ref[rows, :]
        r, k, v, a, b = (ref[rows, :].astype(F32) for ref in (r_ref, k_ref, v_ref, a_ref, b_ref))
        e_cum = jnp.exp(cum)
        e_inv = jnp.exp(-cum)
        e_end = jnp.exp(cum[C - 1:C, :] - cum)
        rt = r * e_cum
        at = (a * jnp.exp(cum - lw)).astype(BF16)
        kt = (k * e_inv).astype(BF16)
        bt = (b * e_inv).astype(BF16)
        kb = jnp.concatenate([stack(kt), stack(bt)], axis=0)
        ga = lax.dot_general(at, kb, _NT, preferred_element_type=F32)
        gr = lax.dot_general(rt.astype(BF16), kb, _NT, preferred_element_type=F32)
        ats_scr[c] = stack(at)
        vs_scr[c] = stack(v.astype(BF16))
        rt_scr[rows, :] = rt
        kend_scr[rows, :] = (k * e_end).astype(BF16)
        bend_scr[rows, :] = (b * e_end).astype(BF16)
        wend_scr[c] = jnp.broadcast_to(e_cum[C - 1:C, :], (8, LANES))
        yield
        a_ab = jnp.where(strict, ga[:, LANES:], 0.0)
        lp_scr[c] = a_ab.astype(BF16)
        t_scr[c] = eye_pair + a_ab
        aak_scr[c] = jnp.where(strict, ga[:, :LANES], 0.0).astype(BF16)
        ark_scr[c] = jnp.where(incl, gr[:, :LANES], 0.0).astype(BF16)
        arb_scr[c] = jnp.where(incl, gr[:, LANES:], 0.0).astype(BF16)

    def square_only(c):
        lp = lp_scr[c]
        lp2 = _dot(lp, stack(lp))
        yield
        lp_scr[c] = lp2.astype(BF16)

    def fold_and_square(c):
        lp = lp_scr[c]
        lp_bd = stack(lp)
        t = t_scr[c]
        tl = _dot(t.astype(BF16), lp_bd)
        lp2 = _dot(lp, lp_bd)
        yield
        t_scr[c] = t + tl
        lp_scr[c] = lp2.astype(BF16)

    def fold_only(c):
        t = t_scr[c]
        tl = _dot(t.astype(BF16), stack(lp_scr[c]))
        yield
        t_scr[c] = t + tl

    def chunk_maps(c):
        rows = chunk_rows(c)
        vs = vs_scr[c]
        x = _dot(aak_scr[c], vs)
        y0a = _dot(ark_scr[c], vs)
        yield
        au = _dot(t_scr[c].astype(BF16),
                  jnp.concatenate([ats_scr[c], stack(x.astype(BF16))], axis=1))
        yield
        ah = au[:, :LANES].astype(BF16)
        u0 = au[:, LANES:].astype(BF16)
        ry = _dot(arb_scr[c], jnp.concatenate([stack(ah), stack(u0)], axis=1))
        k_end = kend_scr[rows, :]
        b_end = bend_scr[rows, :]
        p = lax.dot_general(b_end, ah, _TN, preferred_element_type=F32)
        q = lax.dot_general(jnp.concatenate([k_end, b_end], axis=0),
                            jnp.concatenate([v_ref[rows, :], u0], axis=0), _TN,
                            preferred_element_type=F32)
        yield
        rh_scr[cur, rows, :] = rt_scr[rows, :] + ry[:, :LANES]
        y0_scr[cur, rows, :] = y0a + ry[:, LANES:]
        p_scr[cur, c] = jnp.where(same, p, 0.0) + eye * wend_scr[c][0:1, :]
        q_scr[cur, c] = jnp.where(same, q, 0.0)

    def scan_chunks():
        for c in range(n_chunks):
            rows = chunk_rows(c)
            st = st_scr[...]
            ys = _mm(rh_scr[prev, rows, :], st)
            st_new = _mm_3pass(p_scr[prev, c], st)
            yield
            y_ref[rows, :] = (y0_scr[prev, rows, :] + ys).astype(y_ref.dtype)
            st_scr[...] = st_new + q_scr[prev, c]

    stages = [gram_terms, square_only]
    n = 4
    while n < C:
        stages.append(fold_and_square)
        n *= 2
    stages += [fold_only, chunk_maps]

    @pl.when(step == 0)
    def _():
        rh_scr[1] = jnp.zeros(rh_scr.shape[1:], F32)
        y0_scr[1] = jnp.zeros(y0_scr.shape[1:], F32)
        p_scr[1] = jnp.zeros(p_scr.shape[1:], F32)
        q_scr[1] = jnp.zeros(q_scr.shape[1:], F32)

    def chunk_pipeline(c):
        for stage in stages:
            yield from stage(c)

    st_scr[...] = jnp.zeros((LANES, LANES), F32)
    gens = [chunk_pipeline(c) for c in range(n_chunks)]
    scan = scan_chunks()
    live = set(range(n_chunks))
    t = 0
    scan_live = True
    while live or scan_live:
        for i in sorted(live):
            if t >= i and (t - i) % WKV_PHASE_GAP == 0:
                try:
                    next(gens[i])
                except StopIteration:
                    live.discard(i)
        if scan_live and t % WKV_SCAN_GAP == 0:
            try:
                next(scan)
            except StopIteration:
                scan_live = False
        t += 1


def _wkv(r, lw, k, v, an, bn):
    B, S, D = r.shape
    n_pairs = D // LANES
    n_steps = B * n_pairs
    n_chunks = S // CHUNK
    in_blk = pl.BlockSpec((None, S, LANES), lambda s: (jnp.minimum(s, n_steps - 1) // n_pairs, 0,
                                                       jnp.minimum(s, n_steps - 1) % n_pairs))
    out_blk = pl.BlockSpec((None, S, LANES), lambda s: (jnp.maximum(s - 1, 0) // n_pairs, 0,
                                                        jnp.maximum(s - 1, 0) % n_pairs))
    mat = lambda dt: pltpu.VMEM((n_chunks, LANES, LANES), dt)
    pair = lambda dt: pltpu.VMEM((n_chunks, CHUNK, LANES), dt)
    seq = lambda dt: pltpu.VMEM((S, LANES), dt)
    return pl.pallas_call(
        _wkv_kernel,
        grid=(n_steps + 1,),
        in_specs=[in_blk] * 6,
        out_specs=out_blk,
        out_shape=jax.ShapeDtypeStruct((B, S, D), BF16),
        scratch_shapes=[pair(BF16), pair(F32), pair(BF16), pair(BF16), pair(BF16), mat(BF16), mat(BF16),
                        seq(F32), seq(BF16), seq(BF16), pltpu.VMEM((n_chunks, 8, LANES), F32),
                        pltpu.VMEM((2, S, LANES), F32), pltpu.VMEM((2, S, LANES), F32),
                        pltpu.VMEM((2, n_chunks, LANES, LANES), F32),
                        pltpu.VMEM((2, n_chunks, LANES, LANES), F32),
                        pltpu.VMEM((LANES, LANES), F32)],
        compiler_params=pltpu.CompilerParams(vmem_limit_bytes=VMEM_LIMIT,
                                             dimension_semantics=("arbitrary",)),
        name="wkv",
    )(r, lw, k, v, an, bn)


def _a_post_kernel(y_ref, r_ref, k_ref, v_ref, sg_ref, x_ref, mod_ref, lng_ref, lnb_ref, rk_ref,
                   wout_ref, kvg_ref, wkv_ref, kng_ref, cos_ref, sin_ref,
                   xr_out, ksh_out, vsh_out):
    D = x_ref.shape[-1]
    e = _head_ones()
    n_lane_groups = D // LANES
    f32 = lambda ref: ref[...].astype(F32)
    y = f32(y_ref)
    mean = _head_sum(y, e) * (1.0 / HEAD_DIM)
    d = y - mean
    var = _head_sum(d * d, e) * (1.0 / HEAD_DIM)
    yn = d * lax.rsqrt(var + GN_EPS) * lng_ref[...] + lnb_ref[...]
    bonus = _head_sum(f32(r_ref) * f32(k_ref) * rk_ref[...], e) * f32(v_ref)
    mix = _mm((yn + bonus) * f32(sg_ref), wout_ref[...])
    gate = mod_ref[...][:, 2 * D:]
    xr = x_ref[...] + gate * mix
    xr_out[...] = xr

    kv = _mm(_rms_rows(xr) * kvg_ref[...], wkv_ref[...])
    ks = kv[:, :D]
    ks = ks * lax.rsqrt(_head_sum(ks * ks, e) * (1.0 / HEAD_DIM) + NORM_EPS) * kng_ref[...]
    cos = _tile_lanes(cos_ref[...], n_lane_groups)
    sin = _tile_lanes(sin_ref[...], n_lane_groups)
    ksh_out[...] = (ks * cos + _rot_half(ks) * sin).astype(ksh_out.dtype)
    vsh_out[...] = kv[:, D:].astype(vsh_out.dtype)


def _a_post(y, r, k, v, sg, x, mod3, ln_g, ln_b, r_k, w_out, kv_g, w_kv, kn_g, cos_t, sin_t):
    B, S, D = x.shape
    ts = WIDE_SEQ_TILE
    tile = pl.BlockSpec((None, ts, D), lambda b, i: (b, i, 0))
    vec = pl.BlockSpec((1, D), lambda b, i: (0, 0))
    rope = pl.BlockSpec((ts, LANES), lambda b, i: (i, 0))

    def full(a):
        return pl.BlockSpec(a.shape, lambda b, i: (0,) * a.ndim)

    act = lambda dt: jax.ShapeDtypeStruct((B, S, D), dt)
    return pl.pallas_call(
        _a_post_kernel,
        grid=(B, S // ts),
        in_specs=[tile] * 6 + [pl.BlockSpec((None, 1, 3 * D), lambda b, i: (b, 0, 0)), vec, vec, vec,
                               full(w_out), vec, full(w_kv), vec, rope, rope],
        out_specs=[tile] * 3,
        out_shape=[act(F32), act(BF16), act(BF16)],
        compiler_params=pltpu.CompilerParams(vmem_limit_bytes=VMEM_LIMIT),
        name="a_post",
    )(y, r, k, v, sg, x, mod3, ln_g, ln_b, r_k, w_out, kv_g, w_kv, kn_g, cos_t, sin_t)


def _b_pre_kernel(x_ref, mod_ref, ng_ref, win_ref, qg_ref, cos_ref, sin_ref, q_out, sg_out):
    D = x_ref.shape[-1]
    nq = q_out.shape[-1]
    e = _head_ones()
    mod = mod_ref[...]
    shift, scale = mod[:, :D], mod[:, D:2 * D]
    h = (_rms_rows(x_ref[...]) * (ng_ref[...] * (1.0 + scale)) + shift).astype(BF16)
    cos = _tile_lanes(cos_ref[...], D // LANES)
    sin = _tile_lanes(sin_ref[...], D // LANES)
    for g in range(nq // D):
        q = _dot(h, win_ref[:, g * D:(g + 1) * D])
        q = q * lax.rsqrt(_head_sum(q * q, e) * (1.0 / HEAD_DIM) + NORM_EPS) * qg_ref[...]
        q_out[:, g * D:(g + 1) * D] = ((q * cos + _rot_half(q) * sin) * Q_SCALE).astype(q_out.dtype)
    gate = _dot(h, win_ref[:, nq:])
    sg_out[...] = (gate * _sigmoid(gate)).astype(sg_out.dtype)


def _b_pre(xr, mod3, ng, w_in, qn_g, cos_t, sin_t):
    B, S, D = xr.shape
    nq = w_in.shape[1] - D
    ts = WIDE_SEQ_TILE
    tile = pl.BlockSpec((None, ts, D), lambda b, i: (b, i, 0))
    vec = pl.BlockSpec((1, D), lambda b, i: (0, 0))
    rope = pl.BlockSpec((ts, LANES), lambda b, i: (i, 0))
    return pl.pallas_call(
        _b_pre_kernel,
        grid=(B, S // ts),
        in_specs=[tile, pl.BlockSpec((None, 1, 3 * D), lambda b, i: (b, 0, 0)), vec,
                  pl.BlockSpec(w_in.shape, lambda b, i: (0, 0)), vec, rope, rope],
        out_specs=[pl.BlockSpec((None, ts, nq), lambda b, i: (b, i, 0)), tile],
        out_shape=[jax.ShapeDtypeStruct((B, S, nq), BF16), jax.ShapeDtypeStruct((B, S, D), BF16)],
        compiler_params=pltpu.CompilerParams(vmem_limit_bytes=VMEM_LIMIT),
        name="b_pre",
    )(xr, mod3, ng, w_in, qn_g, cos_t, sin_t)


def _attn_kernel(q0_ref, q1_ref, q2_ref, k_ref, v_ref, o_ref,
                 tmp_scr, qr_scr, kr_scr, vr_scr, acc_scr, m_scr, l_scr, s_scr, p_scr):
    S = k_ref.shape[0]
    blk = BAND_BLOCK
    lane = lax.broadcasted_iota(jnp.int32, (1, LANES), 1)
    head0 = lane < HEAD_DIM

    def head_ones(nk):
        hsel = (lax.broadcasted_iota(jnp.int32, (HEADS_PER_GROUP * nk, LANES), 1) // HEAD_DIM
                == lax.broadcasted_iota(jnp.int32, (HEADS_PER_GROUP * nk, LANES), 0) // nk)
        return hsel.astype(F32).astype(BF16)

    def upcast(src_ref):
        rows_per = 4 * blk

        def tile(t, carry):
            rows = pl.ds(pl.multiple_of(t * rows_per, rows_per), rows_per)
            tmp_scr[rows, :] = src_ref[rows, :].astype(F32)
            return carry
        lax.fori_loop(0, S // rows_per, tile, 0)

    for gi, (_, dil) in enumerate(DIL_GROUPS):
        if dil == 1:
            continue
        seg = S // dil
        for src_ref, dst_scr in ((q1_ref if gi == 1 else q2_ref, qr_scr), (k_ref, kr_scr), (v_ref, vr_scr)):
            upcast(src_ref)
            for rho in range(dil):
                dst_scr[gi - 1, rho * seg:(rho + 1) * seg, :] = (
                    tmp_scr[pl.ds(rho, seg, stride=dil), :].astype(BF16))

    def geometry(gi, rho, n):
        dil = DIL_GROUPS[gi][1]
        j = rho * (S // dil // blk) + n
        lo = (j - 1) * blk if n > 0 else j * blk
        nk = (j + 1) * blk - lo
        if gi == 0:
            srcs = (q0_ref, k_ref, v_ref)
        else:
            srcs = (qr_scr.at[gi - 1], kr_scr.at[gi - 1], vr_scr.at[gi - 1])
        return j, lo, nk, srcs

    def token_rows(gi, rho, n, start, size):
        dil = DIL_GROUPS[gi][1]
        first = rho + dil * (blk * n + start)
        return slice(first, first + size) if dil == 1 else pl.ds(first, size, stride=dil)

    def per_head(x, other):
        return jnp.concatenate([jnp.where(head0, x, other), jnp.where(head0, other, x)], axis=0)

    def scores(b, slot):
        j, lo, nk, (q_src, k_src, _) = geometry(*b)
        kc = k_src[lo:lo + nk, :]
        s_scr[slot, :, :2 * nk] = lax.dot_general(q_src[j * blk:(j + 1) * blk, :],
                                                  per_head(kc, jnp.zeros_like(kc)), _NT,
                                                  preferred_element_type=F32)

    def softmax(b, slot):
        gi = b[0]
        win, dil = DIL_GROUPS[gi]
        nk = geometry(*b)[2]
        rt = blk // 2
        qi = lax.broadcasted_iota(jnp.int32, (rt, nk), 0)
        kj = lax.broadcasted_iota(jnp.int32, (rt, nk), 1)
        for t in range(blk // rt):
            diff = (nk - blk) + (qi + t * rt) - kj
            valid = (diff >= 0) & (diff <= win // dil)
            ms = []
            for h in range(HEADS_PER_GROUP):
                s = jnp.where(valid, s_scr[slot, t * rt:(t + 1) * rt, h * nk:(h + 1) * nk], NEG_INF)
                m = jnp.max(s, axis=-1, keepdims=True)
                p_scr[slot, t * rt:(t + 1) * rt, h * nk:(h + 1) * nk] = jnp.exp2(s - m).astype(BF16)
                ms.append(m)
            m_scr[gi, token_rows(*b, t * rt, rt), :] = jnp.where(head0, ms[0], ms[1])

    def weighted_values(b, slot):
        gi = b[0]
        _, lo, nk, (_, _, v_src) = geometry(*b)
        vc = v_src[lo:lo + nk, :]
        zero = jnp.zeros_like(vc)
        v2 = jnp.concatenate([per_head(vc, zero), head_ones(nk)], axis=1)
        pvl = _dot(p_scr[slot, :, :2 * nk], v2)
        rows = token_rows(*b, 0, blk)
        acc_scr[gi, rows, :] = pvl[:, :LANES]
        l_scr[gi, rows, :] = pvl[:, LANES:]

    blocks = [(gi, rho, n) for gi, (_, dil) in enumerate(DIL_GROUPS)
              for rho in range(dil) for n in range(S // dil // blk)]
    gap = ATTN_PHASE_GAP
    n_slots = s_scr.shape[0]
    assert n_slots > 2 * gap
    for t in range(len(blocks) + 2 * gap):
        if 0 <= t - 2 * gap < len(blocks):
            weighted_values(blocks[t - 2 * gap], (t - 2 * gap) % n_slots)
        if t < len(blocks):
            scores(blocks[t], t % n_slots)
        if 0 <= t - gap < len(blocks):
            softmax(blocks[t - gap], (t - gap) % n_slots)

    def merge(t, carry):
        rows = pl.ds(pl.multiple_of(t * blk, blk), blk)
        ms = [m_scr[g, rows, :] for g in range(len(DIL_GROUPS))]
        m_all = functools.reduce(jnp.maximum, ms)
        ws = [jnp.exp2(m - m_all) for m in ms]
        num = functools.reduce(jnp.add, [w * acc_scr[g, rows, :] for g, w in enumerate(ws)])
        den = functools.reduce(jnp.add, [w * l_scr[g, rows, :] for g, w in enumerate(ws)])
        o_ref[rows, :] = (num / den).astype(o_ref.dtype)
        return carry

    lax.fori_loop(0, S // blk, merge, 0)


def _attn(q, k_sh, v_sh):
    B, S, D = k_sh.shape
    n_groups = len(DIL_GROUPS)
    n_pairs = D // LANES
    qspec = lambda g: pl.BlockSpec((None, S, LANES), lambda b, p: (b, 0, g * n_pairs + p))
    blk = pl.BlockSpec((None, S, LANES), lambda b, p: (b, 0, p))
    res = lambda: pltpu.VMEM((n_groups - 1, S, LANES), BF16)
    nat = lambda: pltpu.VMEM((n_groups, S, LANES), F32)
    return pl.pallas_call(
        _attn_kernel,
        grid=(B, n_pairs),
        in_specs=[qspec(0), qspec(1), qspec(2), blk, blk],
        out_specs=blk,
        out_shape=jax.ShapeDtypeStruct((B, S, D), BF16),
        scratch_shapes=[pltpu.VMEM((S, LANES), F32), res(), res(), res(), nat(), nat(), nat(),
                        pltpu.VMEM((ATTN_SLOTS, BAND_BLOCK, 2 * HEADS_PER_GROUP * BAND_BLOCK), F32),
                        pltpu.VMEM((ATTN_SLOTS, BAND_BLOCK, 2 * HEADS_PER_GROUP * BAND_BLOCK), BF16)],
        compiler_params=pltpu.CompilerParams(vmem_limit_bytes=VMEM_LIMIT),
        name="attn",
    )(q, q, q, k_sh, v_sh)


def _b_post_kernel(att_ref, sg_ref, x_ref, mod_ref, wout_ref, out_ref):
    D = x_ref.shape[-1]
    gate = mod_ref[...][:, 2 * D:]
    y = att_ref[...].astype(F32) * sg_ref[...].astype(F32)
    out_ref[...] = x_ref[...] + gate * _mm(y, wout_ref[...])


def _b_post(att, sg, xr, mod3, w_out):
    B, S, D = xr.shape
    ts = WIDE_SEQ_TILE
    tile = pl.BlockSpec((None, ts, D), lambda b, i: (b, i, 0))
    return pl.pallas_call(
        _b_post_kernel,
        grid=(B, S // ts),
        in_specs=[tile, tile, tile, pl.BlockSpec((None, 1, 3 * D), lambda b, i: (b, 0, 0)),
                  pl.BlockSpec(w_out.shape, lambda b, i: (0, 0))],
        out_specs=tile,
        out_shape=jax.ShapeDtypeStruct((B, S, D), F32),
        compiler_params=pltpu.CompilerParams(vmem_limit_bytes=VMEM_LIMIT),
        name="b_post",
    )(att, sg, xr, mod3, w_out)


def _rope_tables(seq):
    pos = jnp.arange(seq, dtype=F32)
    inv = ROPE_THETA ** (-jnp.arange(0, HEAD_DIM, 2, dtype=F32) / HEAD_DIM)
    ang = pos[:, None] * inv[None, :]
    cos, sin = jnp.cos(ang), jnp.sin(ang)
    cos_t = jnp.concatenate([cos, cos] * HEADS_PER_GROUP, axis=-1)
    sin_t = jnp.concatenate([-sin, sin] * HEADS_PER_GROUP, axis=-1)
    return cos_t, sin_t


def kernel(x, c, a_ada_w, a_ada_b, a_norm_g, a_mix_mu, a_w_in, a_w0, a_w1, a_w2, a_a0, a_a1, a_a2,
           a_k_k, a_k_a, a_r_k, a_ln_g, a_ln_b, a_w_out, kv_norm_g, w_kv, k_norm_g,
           b_ada_w, b_ada_b, b_norm_g, b_w_in, b_q_norm_g, b_w_out):
    B, S, D = x.shape
    assert a_ada_w.shape[0] == 1 and b_ada_w.shape[0] == 1
    assert D % LANES == 0 and S % (BAND_BLOCK * DIL_GROUPS[-1][1]) == 0 and S % WIDE_SEQ_TILE == 0
    assert all(win // dil == BAND_BLOCK for win, dil in DIL_GROUPS)
    assert b_w_in.shape[-1] == (len(DIL_GROUPS) + 1) * D
    row = lambda t: t.reshape(1, -1)
    per_head = lambda t: jnp.tile(t.reshape(1, HEAD_DIM), (1, D // HEAD_DIM))

    mod_a, mod_b = _adaln(c, a_ada_w, a_ada_b, b_ada_w, b_ada_b)
    mod_a = mod_a.reshape(B, 1, 3 * D)
    mod_b = mod_b.reshape(B, 1, 3 * D)
    cos_t, sin_t = _rope_tables(S)

    r, lw, k, v, an, bn, sg_a = _a_pre(
        x, mod_a, a_norm_g, a_mix_mu[0], a_w_in[0].astype(BF16), a_w0, a_w1[0].astype(BF16),
        a_w2[0].astype(BF16), a_a0, a_a1[0].astype(BF16), a_a2[0].astype(BF16), a_k_k, a_k_a)
    y = _wkv(r, lw, k, v, an, bn)
    xr, k_sh, v_sh = _a_post(
        y, r, k, v, sg_a, x, mod_a, a_ln_g, a_ln_b, row(a_r_k[0]), a_w_out[0].astype(BF16),
        row(kv_norm_g), w_kv.astype(BF16), per_head(k_norm_g), cos_t, sin_t)

    q, sg_b = _b_pre(xr, mod_b, b_norm_g, b_w_in[0].astype(BF16), per_head(b_q_norm_g[0]), cos_t, sin_t)
    att = _attn(q, k_sh, v_sh)
    return _b_post(att, sg_b, xr, mod_b, b_w_out[0].astype(BF16))
```

```python
import functools

import jax
import jax.numpy as jnp
from jax import lax
from jax.experimental import pallas as pl
from jax.experimental.pallas import tpu as pltpu

F32 = jnp.float32
BF16 = jnp.bfloat16

HEAD_DIM = 64
LANES = 128
HEADS_PER_GROUP = LANES // HEAD_DIM
DIL_GROUPS = ((128, 1), (512, 4), (2048, 16))
BAND_BLOCK = 128
ROPE_THETA = 10000.0
NORM_EPS = 1e-6
GN_EPS = 64e-5
NEG_INF = -1e30
DECAY_SCALE = 0.6065306597126334
CHUNK = 64
WKV_PHASE_GAP = 4
WKV_SCAN_GAP = 2
ATTN_PHASE_GAP = 2
ATTN_SLOTS = 8
SEQ_TILE = 256
WIDE_SEQ_TILE = 512
Q_SCALE = HEAD_DIM ** -0.5 * 1.4426950408889634
VMEM_LIMIT = 56 * 1024 * 1024

_NT = (((1,), (1,)), ((), ()))
_TN = (((0,), (0,)), ((), ()))


def _dot(a, b):
    return jnp.dot(a, b, preferred_element_type=F32)


def _mm(a, b):
    return _dot(a.astype(BF16), b.astype(BF16))


def _split_dot_lhs(e, x):
    hi = x.astype(BF16)
    lo = (x - hi.astype(F32)).astype(BF16)
    return _dot(e, hi) + _dot(e, lo)


def _mm_3pass(a, b):
    a_hi = a.astype(BF16)
    a_lo = (a - a_hi.astype(F32)).astype(BF16)
    b_hi = b.astype(BF16)
    b_lo = (b - b_hi.astype(F32)).astype(BF16)
    return _dot(a_hi, b_hi) + (_dot(a_hi, b_lo) + _dot(a_lo, b_hi))


def _head_sum(x, e):
    parts = [_dot(x[:, g * LANES:(g + 1) * LANES].astype(BF16), e) for g in range(x.shape[1] // LANES)]
    return parts[0] if len(parts) == 1 else jnp.concatenate(parts, axis=1)


def _rot_half(x):
    lane = lax.broadcasted_iota(jnp.int32, (1, LANES), 1)
    first = (lane % HEAD_DIM) < (HEAD_DIM // 2)
    parts = []
    for g in range(x.shape[1] // LANES):
        xg = x[:, g * LANES:(g + 1) * LANES]
        up = pltpu.roll(xg, LANES - HEAD_DIM // 2, axis=1)
        dn = pltpu.roll(xg, HEAD_DIM // 2, axis=1)
        parts.append(jnp.where(first, up, dn))
    return parts[0] if len(parts) == 1 else jnp.concatenate(parts, axis=1)


def _tile_lanes(t, n):
    return t if n == 1 else jnp.concatenate([t] * n, axis=1)


def _rms_rows(x):
    return x * lax.rsqrt(jnp.mean(x * x, axis=-1, keepdims=True) + NORM_EPS)


def _sigmoid(x):
    return 1.0 / (1.0 + jnp.exp(-x))


def _head_ones():
    r = lax.broadcasted_iota(jnp.int32, (LANES, LANES), 0) // HEAD_DIM
    c = lax.broadcasted_iota(jnp.int32, (LANES, LANES), 1) // HEAD_DIM
    return (r == c).astype(BF16)


def _adaln_kernel(c_ref, wa_ref, ba_ref, wb_ref, bb_ref, oa_ref, ob_ref):
    c = c_ref[...]
    sc = c * _sigmoid(c)
    oa_ref[...] = _mm_3pass(sc, wa_ref[...]) + ba_ref[...]
    ob_ref[...] = _mm_3pass(sc, wb_ref[...]) + bb_ref[...]


def _adaln(c, wa, ba, wb, bb):
    B, D = c.shape
    n3 = wa.shape[-1]
    tn = 1024
    wspec = pl.BlockSpec((None, D, tn), lambda j: (0, 0, j))
    bspec = pl.BlockSpec((1, tn), lambda j: (0, j))
    ospec = pl.BlockSpec((B, tn), lambda j: (0, j))
    return pl.pallas_call(
        _adaln_kernel,
        grid=(n3 // tn,),
        in_specs=[pl.BlockSpec((B, D), lambda j: (0, 0)), wspec, bspec, wspec, bspec],
        out_specs=[ospec, ospec],
        out_shape=[jax.ShapeDtypeStruct((B, n3), F32)] * 2,
        name="adaln",
    )(c, wa, ba, wb, bb)


def _a_pre_kernel(x_ref, xp_ref, mod_ref, ng_ref, mu_ref, win_ref, w0_ref, w1_ref, w2_ref,
                  a0_ref, a1_ref, a2_ref, kk_ref, ka_ref,
                  r_out, lw_out, k_out, v_out, an_out, bn_out, sg_out):
    D = x_ref.shape[-1]
    i = pl.program_id(1)
    mod = mod_ref[...]
    shift, scale = mod[:, :D], mod[:, D:2 * D]
    gain = ng_ref[...] * (1.0 + scale)

    h = _rms_rows(x_ref[...]) * gain + shift
    hp = _rms_rows(xp_ref[...]) * gain + shift
    hp = jnp.where(i == 0, 0.0, hp[7:8, :])
    row = lax.broadcasted_iota(jnp.int32, h.shape, 0)
    hs = jnp.where(row == 0, hp, pltpu.roll(h, 1, axis=0))
    xx = hs - h
    mu = mu_ref[...]
    lerp = lambda p: (h + xx * mu[p:p + 1, :]).astype(BF16)
    wl = _mm(jnp.tanh(_mm(lerp(4), w1_ref[...])), w2_ref[...])
    al = _mm(_mm(lerp(5), a1_ref[...]), a2_ref[...])
    r = _dot(lerp(0), win_ref[:, 0 * D:1 * D])
    k = _dot(lerp(1), win_ref[:, 1 * D:2 * D])
    v = _dot(lerp(2), win_ref[:, 2 * D:3 * D])
    g = _dot(lerp(3), win_ref[:, 3 * D:4 * D])

    a = _sigmoid(a0_ref[...] + al)

    kk = k * kk_ref[...]
    ss = _head_sum(kk * kk, _head_ones())
    kk = kk * lax.rsqrt(jnp.maximum(ss, 1e-24))

    r_out[...] = r.astype(r_out.dtype)
    lw_out[...] = (-DECAY_SCALE) * _sigmoid(w0_ref[...] + wl)
    k_out[...] = (k * (1.0 + (a - 1.0) * ka_ref[...])).astype(k_out.dtype)
    v_out[...] = v.astype(v_out.dtype)
    an_out[...] = (-kk).astype(an_out.dtype)
    bn_out[...] = (kk * a).astype(bn_out.dtype)
    sg_out[...] = (g * _sigmoid(g)).astype(sg_out.dtype)


def _a_pre(x, mod3, ng, mu, w_in, w0, w1, w2, a0, a1, a2, k_k, k_a):
    B, S, D = x.shape
    ts = SEQ_TILE
    tile = pl.BlockSpec((None, ts, D), lambda b, i: (b, i, 0))
    prev = pl.BlockSpec((None, 8, D), lambda b, i: (b, jnp.maximum(i * (ts // 8) - 1, 0), 0))
    vec = pl.BlockSpec((1, D), lambda b, i: (0, 0))

    def full(a):
        return pl.BlockSpec(a.shape, lambda b, i: (0,) * a.ndim)

    act = lambda dt: jax.ShapeDtypeStruct((B, S, D), dt)
    return pl.pallas_call(
        _a_pre_kernel,
        grid=(B, S // ts),
        in_specs=[tile, prev, pl.BlockSpec((None, 1, 3 * D), lambda b, i: (b, 0, 0)), vec, full(mu),
                  full(w_in), vec, full(w1), full(w2), vec, full(a1), full(a2), vec, vec],
        out_specs=[tile] * 7,
        out_shape=[act(BF16), act(F32), act(BF16), act(BF16), act(BF16), act(BF16), act(BF16)],
        compiler_params=pltpu.CompilerParams(vmem_limit_bytes=VMEM_LIMIT),
        name="a_pre",
    )(x, x, mod3, ng, mu, w_in, w0, w1, w2, a0, a1, a2, k_k, k_a)


def _wkv_kernel(r_ref, lw_ref, k_ref, v_ref, a_ref, b_ref, y_ref,
                lp_scr, t_scr, aak_scr, ark_scr, arb_scr, ats_scr, vs_scr, rt_scr, kend_scr, bend_scr,
                wend_scr, rh_scr, y0_scr, p_scr, q_scr, st_scr):
    C = CHUNK
    n_chunks = r_ref.shape[0] // C
    assert C == HEAD_DIM
    step = pl.program_id(0)
    cur = step % 2
    prev = 1 - cur

    lane = lax.broadcasted_iota(jnp.int32, (1, LANES), 1)
    head0 = lane < HEAD_DIM
    ti = lax.broadcasted_iota(jnp.int32, (C, LANES), 0)
    ii = lax.broadcasted_iota(jnp.int32, (C, LANES), 1) % C
    strict = ii < ti
    incl = ii <= ti
    eye_pair = (ii == ti).astype(F32)
    ri = lax.broadcasted_iota(jnp.int32, (LANES, LANES), 0)
    ci = lax.broadcasted_iota(jnp.int32, (LANES, LANES), 1)
    same = (ri // HEAD_DIM) == (ci // HEAD_DIM)
    tri = (lax.broadcasted_iota(jnp.int32, (C, C), 1)
           <= lax.broadcasted_iota(jnp.int32, (C, C), 0)).astype(BF16)

    def stack(x):
        zero = jnp.zeros_like(x)
        return jnp.concatenate([jnp.where(head0, x, zero), jnp.where(head0, zero, x)], axis=0)

    def chunk_rows(c):
        return slice(c * C, (c + 1) * C)

    def gram_terms(c):
        rows = chunk_rows(c)
        cum = _split_dot_lhs(tri, lw_ref[rows, :])
        yield
        lw = lw_ref[rows, :]
        r, k, v, a, b = (ref[rows, :].astype(F32) for ref in (r_ref, k_ref, v_ref, a_ref, b_ref))
        e_cum = jnp.exp(cum)
        e_inv = jnp.exp(-cum)
        e_end = jnp.exp(cum[C - 1:C, :] - cum)
        rt = r * e_cum
        at = (a * jnp.exp(cum - lw)).astype(BF16)
        kt = (k * e_inv).astype(BF16)
        bt = (b * e_inv).astype(BF16)
        kb = jnp.concatenate([stack(kt), stack(bt)], axis=0)
        ga = lax.dot_general(at, kb, _NT, preferred_element_type=F32)
        gr = lax.dot_general(rt.astype(BF16), kb, _NT, preferred_element_type=F32)
        ats_scr[c] = stack(at)
        vs_scr[c] = stack(v.astype(BF16))
        rt_scr[rows, :] = rt
        kend_scr[rows, :] = (k * e_end).astype(BF16)
        bend_scr[rows, :] = (b * e_end).astype(BF16)
        wend_scr[c] = jnp.broadcast_to(e_cum[C - 1:C, :], (8, LANES))
        yield
        a_ab = jnp.where(strict, ga[:, LANES:], 0.0)
        lp_scr[c] = a_ab.astype(BF16)
        t_scr[c] = eye_pair + a_ab
        aak_scr[c] = jnp.where(strict, ga[:, :LANES], 0.0).astype(BF16)
        ark_scr[c] = jnp.where(incl, gr[:, :LANES], 0.0).astype(BF16)
        arb_scr[c] = jnp.where(incl, gr[:, LANES:], 0.0).astype(BF16)

    def square_only(c):
        lp = lp_scr[c]
        lp2 = _dot(lp, stack(lp))
        yield
        lp_scr[c] = lp2.astype(BF16)

    def fold_and_square(c):
        lp = lp_scr[c]
        lp_bd = stack(lp)
        t = t_scr[c]
        tl = _dot(t.astype(BF16), lp_bd)
        lp2 = _dot(lp, lp_bd)
        yield
        t_scr[c] = t + tl
        lp_scr[c] = lp2.astype(BF16)

    def fold_only(c):
        t = t_scr[c]
        tl = _dot(t.astype(BF16), stack(lp_scr[c]))
        yield
        t_scr[c] = t + tl

    def chunk_maps(c):
        rows = chunk_rows(c)
        vs = vs_scr[c]
        x = _dot(aak_scr[c], vs)
        y0a = _dot(ark_scr[c], vs)
        yield
        au = _dot(t_scr[c].astype(BF16),
                  jnp.concatenate([ats_scr[c], stack(x.astype(BF16))], axis=1))
        yield
        ah = au[:, :LANES].astype(BF16)
        u0 = au[:, LANES:].astype(BF16)
        ry = _dot(arb_scr[c], jnp.concatenate([stack(ah), stack(u0)], axis=1))
        k_end = kend_scr[rows, :]
        b_end = bend_scr[rows, :]
        p = lax.dot_general(b_end, ah, _TN, preferred_element_type=F32)
        q = lax.dot_general(jnp.concatenate([k_end, b_end], axis=0),
                            jnp.concatenate([v_ref[rows, :], u0], axis=0), _TN,
                            preferred_element_type=F32)
        yield
        rh_scr[cur, rows, :] = rt_scr[rows, :] + ry[:, :LANES]
        y0_scr[cur, rows, :] = y0a + ry[:, LANES:]
        p = jnp.where(same, p, 0.0)
        p_pair = p[:C, :] + p[C:, :] + eye_pair * wend_scr[c][0:1, :]
        p_hi = p_pair.astype(BF16)
        p_scr[cur, c] = jnp.concatenate([p_hi, (p_pair - p_hi.astype(F32)).astype(BF16)], axis=1)
        q = jnp.where(same, q, 0.0)
        q_scr[cur, c] = q[:C, :] + q[C:, :]

    def scan_chunks():
        for c in range(n_chunks):
            rows = chunk_rows(c)
            st = st_scr[...]
            st_hi = st.astype(BF16)
            st_lo = (st - st_hi.astype(F32)).astype(BF16)
            hi_bd = stack(st_hi)
            pc = p_scr[prev, c]
            p_hi = pc[:, :LANES]
            st_new = _dot(p_hi, hi_bd) + (_dot(p_hi, stack(st_lo)) + _dot(pc[:, LANES:], hi_bd))
            ys = _dot(rh_scr[prev, rows, :].astype(BF16), hi_bd)
            yield
            y_ref[rows, :] = (y0_scr[prev, rows, :] + ys).astype(y_ref.dtype)
            st_scr[...] = st_new + q_scr[prev, c]

    stages = [gram_terms, square_only]
    n = 4
    while n < C:
        stages.append(fold_and_square)
        n *= 2
    stages += [fold_only, chunk_maps]

    @pl.when(step == 0)
    def _():
        rh_scr[1] = jnp.zeros(rh_scr.shape[1:], F32)
        y0_scr[1] = jnp.zeros(y0_scr.shape[1:], F32)
        p_scr[1] = jnp.zeros(p_scr.shape[1:], BF16)
        q_scr[1] = jnp.zeros(q_scr.shape[1:], F32)

    def chunk_pipeline(c):
        for stage in stages:
            yield from stage(c)

    st_scr[...] = jnp.zeros((C, LANES), F32)
    gens =[chunk_pipeline(c) for c in range(n_chunks)]
    scan = scan_chunks()
    live = set(range(n_chunks))
    t = 0
    scan_live = True
    while live or scan_live:
        for i in sorted(live):
            if t >= i and (t - i) % WKV_PHASE_GAP == 0:
                try:
                    next(gens[i])
                except StopIteration:
                    live.discard(i)
        if scan_live and t % WKV_SCAN_GAP == 0:
            try:
                next(scan)
            except StopIteration:
                scan_live = False
        t += 1


def _wkv(r, lw, k, v, an, bn):
    B, S, D = r.shape
    n_pairs = D // LANES
    n_steps = B * n_pairs
    n_chunks = S // CHUNK
    in_blk = pl.BlockSpec((None, S, LANES), lambda s: (jnp.minimum(s, n_steps - 1) // n_pairs, 0,
                                                       jnp.minimum(s, n_steps - 1) % n_pairs))
    out_blk = pl.BlockSpec((None, S, LANES), lambda s: (jnp.maximum(s - 1, 0) // n_pairs, 0,
                                                        jnp.maximum(s - 1, 0) % n_pairs))
    mat = lambda dt: pltpu.VMEM((n_chunks, LANES, LANES), dt)
    pair = lambda dt: pltpu.VMEM((n_chunks, CHUNK, LANES), dt)
    seq = lambda dt: pltpu.VMEM((S, LANES), dt)
    return pl.pallas_call(
        _wkv_kernel,
        grid=(n_steps + 1,),
        in_specs=[in_blk] * 6,
        out_specs=out_blk,
        out_shape=jax.ShapeDtypeStruct((B, S, D), BF16),
        scratch_shapes=[pair(BF16), pair(F32), pair(BF16), pair(BF16), pair(BF16), mat(BF16), mat(BF16),
                        seq(F32), seq(BF16), seq(BF16), pltpu.VMEM((n_chunks, 8, LANES), F32),
                        pltpu.VMEM((2, S, LANES), F32), pltpu.VMEM((2, S, LANES), F32),
                        pltpu.VMEM((2, n_chunks, CHUNK, 2 * LANES), BF16),
                        pltpu.VMEM((2, n_chunks, CHUNK, LANES), F32),
                        pltpu.VMEM((CHUNK, LANES), F32)],
        compiler_params=pltpu.CompilerParams(vmem_limit_bytes=VMEM_LIMIT,
                                             dimension_semantics=("arbitrary",)),
        name="wkv",
    )(r, lw, k, v, an, bn)


def _a_post_kernel(y_ref, r_ref, k_ref, v_ref, sg_ref, x_ref, mod_ref, lng_ref, lnb_ref, rk_ref,
                   wout_ref, kvg_ref, wkv_ref, kng_ref, cos_ref, sin_ref,
                   xr_out, ksh_out, vsh_out):
    D = x_ref.shape[-1]
    e = _head_ones()
    n_lane_groups = D // LANES
    f32 = lambda ref: ref[...].astype(F32)
    y = f32(y_ref)
    mean = _head_sum(y, e) * (1.0 / HEAD_DIM)
    d = y - mean
    var = _head_sum(d * d, e) * (1.0 / HEAD_DIM)
    yn = d * lax.rsqrt(var + GN_EPS) * lng_ref[...] + lnb_ref[...]
    bonus = _head_sum(f32(r_ref) * f32(k_ref) * rk_ref[...], e) * f32(v_ref)
    mix = _mm((yn + bonus) * f32(sg_ref), wout_ref[...])
    gate = mod_ref[...][:, 2 * D:]
    xr = x_ref[...] + gate * mix
    xr_out[...] = xr

    kv = _mm(_rms_rows(xr) * kvg_ref[...], wkv_ref[...])
    ks = kv[:, :D]
    ks = ks * lax.rsqrt(_head_sum(ks * ks, e) * (1.0 / HEAD_DIM) + NORM_EPS) * kng_ref[...]
    cos = _tile_lanes(cos_ref[...], n_lane_groups)
    sin = _tile_lanes(sin_ref[...], n_lane_groups)
    ksh_out[...] = (ks * cos + _rot_half(ks) * sin).astype(ksh_out.dtype)
    vsh_out[...] = kv[:, D:].astype(vsh_out.dtype)


def _a_post(y, r, k, v, sg, x, mod3, ln_g, ln_b, r_k, w_out, kv_g, w_kv, kn_g, cos_t, sin_t):
    B, S, D = x.shape
    ts = WIDE_SEQ_TILE
    tile = pl.BlockSpec((None, ts, D), lambda b, i: (b, i, 0))
    vec = pl.BlockSpec((1, D), lambda b, i: (0, 0))
    rope = pl.BlockSpec((ts, LANES), lambda b, i: (i, 0))

    def full(a):
        return pl.BlockSpec(a.shape, lambda b, i: (0,) * a.ndim)

    act = lambda dt: jax.ShapeDtypeStruct((B, S, D), dt)
    return pl.pallas_call(
        _a_post_kernel,
        grid=(B, S // ts),
        in_specs=[tile] * 6 + [pl.BlockSpec((None, 1, 3 * D), lambda b, i: (b, 0, 0)), vec, vec, vec,
                               full(w_out), vec, full(w_kv), vec, rope, rope],
        out_specs=[tile] * 3,
        out_shape=[act(F32), act(BF16), act(BF16)],
        compiler_params=pltpu.CompilerParams(vmem_limit_bytes=VMEM_LIMIT),
        name="a_post",
    )(y, r, k, v, sg, x, mod3, ln_g, ln_b, r_k, w_out, kv_g, w_kv, kn_g, cos_t, sin_t)


def _b_pre_kernel(x_ref, mod_ref, ng_ref, win_ref, qg_ref, cos_ref, sin_ref, q_out, sg_out):
    D = x_ref.shape[-1]
    nq = q_out.shape[-1]
    e = _head_ones()
    mod = mod_ref[...]
    shift, scale = mod[:, :D], mod[:, D:2 * D]
    h = (_rms_rows(x_ref[...]) * (ng_ref[...] * (1.0 + scale)) + shift).astype(BF16)
    cos = _tile_lanes(cos_ref[...], D // LANES)
    sin = _tile_lanes(sin_ref[...], D // LANES)
    for g in range(nq // D):
        q = _dot(h, win_ref[:, g * D:(g + 1) * D])
        q = q * lax.rsqrt(_head_sum(q * q, e) * (1.0 / HEAD_DIM) + NORM_EPS) * qg_ref[...]
        q_out[:, g * D:(g + 1) * D] = ((q * cos + _rot_half(q) * sin) * Q_SCALE).astype(q_out.dtype)
    gate = _dot(h, win_ref[:, nq:])
    sg_out[...] = (gate * _sigmoid(gate)).astype(sg_out.dtype)


def _b_pre(xr, mod3, ng, w_in, qn_g, cos_t, sin_t):
    B, S, D = xr.shape
    nq = w_in.shape[1] - D
    ts = WIDE_SEQ_TILE
    tile = pl.BlockSpec((None, ts, D), lambda b, i: (b, i, 0))
    vec = pl.BlockSpec((1, D), lambda b, i: (0, 0))
    rope = pl.BlockSpec((ts, LANES), lambda b, i: (i, 0))
    return pl.pallas_call(
        _b_pre_kernel,
        grid=(B, S // ts),
        in_specs=[tile, pl.BlockSpec((None, 1, 3 * D), lambda b, i: (b, 0, 0)), vec,
                  pl.BlockSpec(w_in.shape, lambda b, i: (0, 0)), vec, rope, rope],
        out_specs=[pl.BlockSpec((None, ts, nq), lambda b, i: (b, i, 0)), tile],
        out_shape=[jax.ShapeDtypeStruct((B, S, nq), BF16), jax.ShapeDtypeStruct((B, S, D), BF16)],
        compiler_params=pltpu.CompilerParams(vmem_limit_bytes=VMEM_LIMIT),
        name="b_pre",
    )(xr, mod3, ng, w_in, qn_g, cos_t, sin_t)


def _attn_kernel(q0_ref, q1_ref, q2_ref, k_ref, v_ref, o_ref,
                 tmp_scr, qr_scr, kr_scr, vr_scr, acc_scr, m_scr, l_scr, s_scr, p_scr):
    S = k_ref.shape[0]
    blk = BAND_BLOCK
    lane = lax.broadcasted_iota(jnp.int32, (1, LANES), 1)
    head0 = lane < HEAD_DIM

    def head_ones(nk):
        hsel = (lax.broadcasted_iota(jnp.int32, (HEADS_PER_GROUP * nk, LANES), 1) // HEAD_DIM
                == lax.broadcasted_iota(jnp.int32, (HEADS_PER_GROUP * nk, LANES), 0) // nk)
        return hsel.astype(F32).astype(BF16)

    def upcast(src_ref):
        rows_per = 4 * blk

        def tile(t, carry):
            rows = pl.ds(pl.multiple_of(t * rows_per, rows_per), rows_per)
            tmp_scr[rows, :] = src_ref[rows, :].astype(F32)
            return carry
        lax.fori_loop(0, S // rows_per, tile, 0)

    for gi, (_, dil) in enumerate(DIL_GROUPS):
        if dil == 1:
            continue
        seg = S // dil
        for src_ref, dst_scr in ((q1_ref if gi == 1 else q2_ref, qr_scr), (k_ref, kr_scr), (v_ref, vr_scr)):
            upcast(src_ref)
            for rho in range(dil):
                dst_scr[gi - 1, rho * seg:(rho + 1) * seg, :] = (
                    tmp_scr[pl.ds(rho, seg, stride=dil), :].astype(BF16))

    def geometry(gi, rho, n):
        dil = DIL_GROUPS[gi][1]
        j = rho * (S // dil // blk) + n
        lo = (j - 1) * blk if n > 0 else j * blk
        nk = (j + 1) * blk - lo
        if gi == 0:
            srcs = (q0_ref, k_ref, v_ref)
        else:
            srcs = (qr_scr.at[gi - 1], kr_scr.at[gi - 1], vr_scr.at[gi - 1])
        return j, lo, nk, srcs

    def token_rows(gi, rho, n, start, size):
        dil = DIL_GROUPS[gi][1]
        first = rho + dil * (blk * n + start)
        return slice(first, first + size) if dil == 1 else pl.ds(first, size, stride=dil)

    def per_head(x, other):
        return jnp.concatenate([jnp.where(head0, x, other), jnp.where(head0, other, x)], axis=0)

    def scores(b, slot):
        j, lo, nk, (q_src, k_src, _) = geometry(*b)
        kc = k_src[lo:lo + nk, :]
        s_scr[slot, :, :2 * nk] = lax.dot_general(q_src[j * blk:(j + 1) * blk, :],
                                                  per_head(kc, jnp.zeros_like(kc)), _NT,
                                                  preferred_element_type=F32)

    def softmax(b, slot):
        gi = b[0]
        win, dil = DIL_GROUPS[gi]
        nk = geometry(*b)[2]
        rt = blk // 2
        qi = lax.broadcasted_iota(jnp.int32, (rt, nk), 0)
        kj = lax.broadcasted_iota(jnp.int32, (rt, nk), 1)
        for t in range(blk // rt):
            diff = (nk - blk) + (qi + t * rt) - kj
            valid = (diff >= 0) & (diff <= win // dil)
            ms = []
            for h in range(HEADS_PER_GROUP):
                s = jnp.where(valid, s_scr[slot, t * rt:(t + 1) * rt, h * nk:(h + 1) * nk], NEG_INF)
                m = jnp.max(s, axis=-1, keepdims=True)
                p_scr[slot, t * rt:(t + 1) * rt, h * nk:(h + 1) * nk] = jnp.exp2(s - m).astype(BF16)
                ms.append(m)
            m_scr[gi, token_rows(*b, t * rt, rt), :] = jnp.where(head0, ms[0], ms[1])

    def weighted_values(b, slot):
        gi = b[0]
        _, lo, nk, (_, _, v_src) = geometry(*b)
        vc = v_src[lo:lo + nk, :]
        zero = jnp.zeros_like(vc)
        v2 = jnp.concatenate([per_head(vc, zero), head_ones(nk)], axis=1)
        pvl = _dot(p_scr[slot, :, :2 * nk], v2)
        rows = token_rows(*b, 0, blk)
        acc_scr[gi, rows, :] = pvl[:, :LANES]
        l_scr[gi, rows, :] = pvl[:, LANES:]

    blocks = [(gi, rho, n) for gi, (_, dil) in enumerate(DIL_GROUPS)
              for rho in range(dil) for n in range(S // dil // blk)]
    gap = ATTN_PHASE_GAP
    n_slots = s_scr.shape[0]
    assert n_slots > 2 * gap
    for t in range(len(blocks) + 2 * gap):
        if 0 <= t - 2 * gap < len(blocks):
            weighted_values(blocks[t - 2 * gap], (t - 2 * gap) % n_slots)
        if t < len(blocks):
            scores(blocks[t], t % n_slots)
        if 0 <= t - gap < len(blocks):
            softmax(blocks[t - gap], (t - gap) % n_slots)

    def merge(t, carry):
        rows = pl.ds(pl.multiple_of(t * blk, blk), blk)
        ms = [m_scr[g, rows, :] for g in range(len(DIL_GROUPS))]
        m_all = functools.reduce(jnp.maximum, ms)
        ws = [jnp.exp2(m - m_all) for m in ms]
        num = functools.reduce(jnp.add, [w * acc_scr[g, rows, :] for g, w in enumerate(ws)])
        den = functools.reduce(jnp.add, [w * l_scr[g, rows, :] for g, w in enumerate(ws)])
        o_ref[rows, :] = (num / den).astype(o_ref.dtype)
        return carry

    lax.fori_loop(0, S // blk, merge, 0)


def _attn(q, k_sh, v_sh):
    B, S, D = k_sh.shape
    n_groups = len(DIL_GROUPS)
    n_pairs = D // LANES
    qspec = lambda g: pl.BlockSpec((None, S, LANES), lambda b, p: (b, 0, g * n_pairs + p))
    blk = pl.BlockSpec((None, S, LANES), lambda b, p: (b, 0, p))
    res = lambda: pltpu.VMEM((n_groups - 1, S, LANES), BF16)
    nat = lambda: pltpu.VMEM((n_groups, S, LANES), F32)
    return pl.pallas_call(
        _attn_kernel,
        grid=(B, n_pairs),
        in_specs=[qspec(0), qspec(1), qspec(2), blk, blk],
        out_specs=blk,
        out_shape=jax.ShapeDtypeStruct((B, S, D), BF16),
        scratch_shapes=[pltpu.VMEM((S, LANES), F32), res(), res(), res(), nat(), nat(), nat(),
                        pltpu.VMEM((ATTN_SLOTS, BAND_BLOCK, 2 * HEADS_PER_GROUP * BAND_BLOCK), F32),
                        pltpu.VMEM((ATTN_SLOTS, BAND_BLOCK, 2 * HEADS_PER_GROUP * BAND_BLOCK), BF16)],
        compiler_params=pltpu.CompilerParams(vmem_limit_bytes=VMEM_LIMIT),
        name="attn",
    )(q, q, q, k_sh, v_sh)


def _b_post_kernel(att_ref, sg_ref, x_ref, mod_ref, wout_ref, out_ref):
    D = x_ref.shape[-1]
    gate = mod_ref[...][:, 2 * D:]
    y = att_ref[...].astype(F32) * sg_ref[...].astype(F32)
    out_ref[...] = x_ref[...] + gate * _mm(y, wout_ref[...])


def _b_post(att, sg, xr, mod3, w_out):
    B, S, D = xr.shape
    ts = WIDE_SEQ_TILE
    tile = pl.BlockSpec((None, ts, D), lambda b, i: (b, i, 0))
    return pl.pallas_call(
        _b_post_kernel,
        grid=(B, S // ts),
        in_specs=[tile, tile, tile, pl.BlockSpec((None, 1, 3 * D), lambda b, i: (b, 0, 0)),
                  pl.BlockSpec(w_out.shape, lambda b, i: (0, 0))],
        out_specs=tile,
        out_shape=jax.ShapeDtypeStruct((B, S, D), F32),
        compiler_params=pltpu.CompilerParams(vmem_limit_bytes=VMEM_LIMIT),
        name="b_post",
    )(att, sg, xr, mod3, w_out)


def _rope_tables(seq):
    pos = jnp.arange(seq, dtype=F32)
    inv = ROPE_THETA ** (-jnp.arange(0, HEAD_DIM, 2, dtype=F32) / HEAD_DIM)
    ang = pos[:, None] * inv[None, :]
    cos, sin = jnp.cos(ang), jnp.sin(ang)
    cos_t = jnp.concatenate([cos, cos] * HEADS_PER_GROUP, axis=-1)
    sin_t = jnp.concatenate([-sin, sin] * HEADS_PER_GROUP, axis=-1)
    return cos_t, sin_t


def kernel(x, c, a_ada_w, a_ada_b, a_norm_g, a_mix_mu, a_w_in, a_w0, a_w1, a_w2, a_a0, a_a1, a_a2,
           a_k_k, a_k_a, a_r_k, a_ln_g, a_ln_b, a_w_out, kv_norm_g, w_kv, k_norm_g,
           b_ada_w, b_ada_b, b_norm_g, b_w_in, b_q_norm_g, b_w_out):
    B, S, D = x.shape
    assert a_ada_w.shape[0] == 1 and b_ada_w.shape[0] == 1
    assert D % LANES == 0 and S % (BAND_BLOCK * DIL_GROUPS[-1][1]) == 0 and S % WIDE_SEQ_TILE == 0
    assert all(win // dil == BAND_BLOCK for win, dil in DIL_GROUPS)
    assert b_w_in.shape[-1] == (len(DIL_GROUPS) + 1) * D
    row = lambda t: t.reshape(1, -1)
    per_head = lambda t: jnp.tile(t.reshape(1, HEAD_DIM), (1, D // HEAD_DIM))

    mod_a, mod_b = _adaln(c, a_ada_w, a_ada_b, b_ada_w, b_ada_b)
    mod_a = mod_a.reshape(B, 1, 3 * D)
    mod_b = mod_b.reshape(B, 1, 3 * D)
    cos_t, sin_t = _rope_tables(S)

    r, lw, k, v, an, bn, sg_a = _a_pre(
        x, mod_a, a_norm_g, a_mix_mu[0], a_w_in[0].astype(BF16), a_w0, a_w1[0].astype(BF16),
        a_w2[0].astype(BF16), a_a0, a_a1[0].astype(BF16), a_a2[0].astype(BF16), a_k_k, a_k_a)
    y = _wkv(r, lw, k, v, an, bn)
    xr, k_sh, v_sh = _a_post(
        y, r, k, v, sg_a, x, mod_a, a_ln_g, a_ln_b, row(a_r_k[0]), a_w_out[0].astype(BF16),
        row(kv_norm_g), w_kv.astype(BF16), per_head(k_norm_g), cos_t, sin_t)

    q, sg_b = _b_pre(xr, mod_b, b_norm_g, b_w_in[0].astype(BF16), per_head(b_q_norm_g[0]), cos_t, sin_t)
    att = _attn(q, k_sh, v_sh)
    return _b_post(att, sg_b, xr, mod_b, b_w_out[0].astype(BF16))
```

```python
import functools

import jax
import jax.numpy as jnp
from jax import lax
from jax.experimental import pallas as pl
from jax.experimental.pallas import tpu as pltpu

F32 = jnp.float32
BF16 = jnp.bfloat16

HEAD_DIM = 64
LANES = 128
HEADS_PER_GROUP = LANES // HEAD_DIM
DIL_GROUPS = ((128, 1), (512, 4), (2048, 16))
BAND_BLOCK = 128
ROPE_THETA = 10000.0
NORM_EPS = 1e-6
GN_EPS = 64e-5
NEG_INF = -1e30
DECAY_SCALE = 0.6065306597126334
CHUNK = 64
WKV_PHASE_GAP = 4
WKV_SCAN_GAP = 2
ATTN_PHASE_GAP = 2
ATTN_SLOTS = 8
SEQ_TILE = 512
Q_SCALE = HEAD_DIM ** -0.5 * 1.4426950408889634
VMEM_LIMIT = 56 * 1024 * 1024

_NT = (((1,), (1,)), ((), ()))
_TN = (((0,), (0,)), ((), ()))


def _dot(a, b):
    return jnp.dot(a, b, preferred_element_type=F32)


def _mm(a, b):
    return _dot(a.astype(BF16), b.astype(BF16))


def _split_dot_lhs(e, x):
    hi = x.astype(BF16)
    lo = (x - hi.astype(F32)).astype(BF16)
    return _dot(e, hi) + _dot(e, lo)


def _mm_3pass(a, b):
    a_hi = a.astype(BF16)
    a_lo = (a - a_hi.astype(F32)).astype(BF16)
    b_hi = b.astype(BF16)
    b_lo = (b - b_hi.astype(F32)).astype(BF16)
    return _dot(a_hi, b_hi) + (_dot(a_hi, b_lo) + _dot(a_lo, b_hi))


def _head_sum(x, e):
    parts = [_dot(x[:, g * LANES:(g + 1) * LANES].astype(BF16), e) for g in range(x.shape[1] // LANES)]
    return parts[0] if len(parts) == 1 else jnp.concatenate(parts, axis=1)


def _rot_half(x):
    lane = lax.broadcasted_iota(jnp.int32, (1, LANES), 1)
    first = (lane % HEAD_DIM) < (HEAD_DIM // 2)
    parts = []
    for g in range(x.shape[1] // LANES):
        xg = x[:, g * LANES:(g + 1) * LANES]
        up = pltpu.roll(xg, LANES - HEAD_DIM // 2, axis=1)
        dn = pltpu.roll(xg, HEAD_DIM // 2, axis=1)
        parts.append(jnp.where(first, up, dn))
    return parts[0] if len(parts) == 1 else jnp.concatenate(parts, axis=1)


def _tile_lanes(t, n):
    return t if n == 1 else jnp.concatenate([t] * n, axis=1)


def _rms_rows(x):
    return x * lax.rsqrt(jnp.mean(x * x, axis=-1, keepdims=True) + NORM_EPS)


def _sigmoid(x):
    return 1.0 / (1.0 + jnp.exp(-x))


def _head_ones():
    r = lax.broadcasted_iota(jnp.int32, (LANES, LANES), 0) // HEAD_DIM
    c = lax.broadcasted_iota(jnp.int32, (LANES, LANES), 1) // HEAD_DIM
    return (r == c).astype(BF16)


def _adaln_kernel(c_ref, wa_ref, ba_ref, wb_ref, bb_ref, oa_ref, ob_ref):
    c = c_ref[...]
    sc = c * _sigmoid(c)
    oa_ref[...] = _mm_3pass(sc, wa_ref[...]) + ba_ref[...]
    ob_ref[...] = _mm_3pass(sc, wb_ref[...]) + bb_ref[...]


def _adaln(c, wa, ba, wb, bb):
    B, D = c.shape
    n3 = wa.shape[-1]
    tn = 512
    wspec = pl.BlockSpec((None, D, tn), lambda j: (0, 0, j))
    bspec = pl.BlockSpec((1, tn), lambda j: (0, j))
    ospec = pl.BlockSpec((B, tn), lambda j: (0, j))
    return pl.pallas_call(
        _adaln_kernel,
        grid=(n3 // tn,),
        in_specs=[pl.BlockSpec((B, D), lambda j: (0, 0)), wspec, bspec, wspec, bspec],
        out_specs=[ospec, ospec],
        out_shape=[jax.ShapeDtypeStruct((B, n3), F32)] * 2,
        name="adaln",
    )(c, wa, ba, wb, bb)


def _a_pre_kernel(x_ref, xp_ref, mod_ref, ng_ref, mu_ref, win_ref, w0_ref, w1_ref, w2_ref,
                  a0_ref, a1_ref, a2_ref, kk_ref, ka_ref,
                  r_out, lw_out, k_out, v_out, an_out, bn_out, sg_out):
    D = x_ref.shape[-1]
    i = pl.program_id(1)
    mod = mod_ref[...]
    shift, scale = mod[:, :D], mod[:, D:2 * D]
    gain = ng_ref[...] * (1.0 + scale)

    h = _rms_rows(x_ref[...]) * gain + shift
    hp = _rms_rows(xp_ref[...]) * gain + shift
    hp = jnp.where(i == 0, 0.0, hp[7:8, :])
    row = lax.broadcasted_iota(jnp.int32, h.shape, 0)
    hs = jnp.where(row == 0, hp, pltpu.roll(h, 1, axis=0))
    xx = hs - h
    mu = mu_ref[...]
    lerp = lambda p: (h + xx * mu[p:p + 1, :]).astype(BF16)
    wl = _mm(jnp.tanh(_mm(lerp(4), w1_ref[...])), w2_ref[...])
    al = _mm(_mm(lerp(5), a1_ref[...]), a2_ref[...])
    r = _dot(lerp(0), win_ref[:, 0 * D:1 * D])
    k = _dot(lerp(1), win_ref[:, 1 * D:2 * D])
    v = _dot(lerp(2), win_ref[:, 2 * D:3 * D])
    g = _dot(lerp(3), win_ref[:, 3 * D:4 * D])

    a = _sigmoid(a0_ref[...] + al)

    kk = k * kk_ref[...]
    ss = _head_sum(kk * kk, _head_ones())
    kk = kk * lax.rsqrt(jnp.maximum(ss, 1e-24))

    r_out[...] = r.astype(r_out.dtype)
    lw_out[...] = (-DECAY_SCALE) * _sigmoid(w0_ref[...] + wl)
    k_out[...] = (k * (1.0 + (a - 1.0) * ka_ref[...])).astype(k_out.dtype)
    v_out[...] = v.astype(v_out.dtype)
    an_out[...] = (-kk).astype(an_out.dtype)
    bn_out[...] = (kk * a).astype(bn_out.dtype)
    sg_out[...] = (g * _sigmoid(g)).astype(sg_out.dtype)


def _a_pre(x, mod3, ng, mu, w_in, w0, w1, w2, a0, a1, a2, k_k, k_a):
    B, S, D = x.shape
    ts = SEQ_TILE
    tile = pl.BlockSpec((None, ts, D), lambda b, i: (b, i, 0))
    prev = pl.BlockSpec((None, 8, D), lambda b, i: (b, jnp.maximum(i * (ts // 8) - 1, 0), 0))
    vec = pl.BlockSpec((1, D), lambda b, i: (0, 0))

    def full(a):
        return pl.BlockSpec(a.shape, lambda b, i: (0,) * a.ndim, pipeline_mode=pl.Buffered(1))

    act = lambda dt: jax.ShapeDtypeStruct((B, S, D), dt)
    return pl.pallas_call(
        _a_pre_kernel,
        grid=(B, S // ts),
        in_specs=[tile, prev, pl.BlockSpec((None, 1, 3 * D), lambda b, i: (b, 0, 0)), vec, full(mu),
                  full(w_in), vec, full(w1), full(w2), vec, full(a1), full(a2), vec, vec],
        out_specs=[tile] * 7,
        out_shape=[act(BF16), act(F32), act(BF16), act(BF16), act(BF16), act(BF16), act(BF16)],
        compiler_params=pltpu.CompilerParams(vmem_limit_bytes=VMEM_LIMIT),
        name="a_pre",
    )(x, x, mod3, ng, mu, w_in, w0, w1, w2, a0, a1, a2, k_k, k_a)


def _wkv_kernel(r_ref, lw_ref, k_ref, v_ref, a_ref, b_ref, y_ref,
                lp_scr, t_scr, aak_scr, ark_scr, arb_scr, ats_scr, vs_scr, rt_scr, kend_scr, bend_scr,
                wend_scr, rh_scr, y0_scr, p_scr, q_scr, st_scr):
    C = CHUNK
    n_chunks = r_ref.shape[0] // C
    assert C == HEAD_DIM
    step = pl.program_id(0)
    cur = step % 2
    prev = 1 - cur

    lane = lax.broadcasted_iota(jnp.int32, (1, LANES), 1)
    head0 = lane < HEAD_DIM
    ti = lax.broadcasted_iota(jnp.int32, (C, LANES), 0)
    ii = lax.broadcasted_iota(jnp.int32, (C, LANES), 1) % C
    strict = ii < ti
    incl = ii <= ti
    eye_pair = (ii == ti).astype(F32)
    ri = lax.broadcasted_iota(jnp.int32, (LANES, LANES), 0)
    ci = lax.broadcasted_iota(jnp.int32, (LANES, LANES), 1)
    same = (ri // HEAD_DIM) == (ci // HEAD_DIM)
    tri = (lax.broadcasted_iota(jnp.int32, (C, C), 1)
           <= lax.broadcasted_iota(jnp.int32, (C, C), 0)).astype(BF16)

    def stack(x):
        zero = jnp.zeros_like(x)
        return jnp.concatenate([jnp.where(head0, x, zero), jnp.where(head0, zero, x)], axis=0)

    def chunk_rows(c):
        return slice(c * C, (c + 1) * C)

    def gram_terms(c):
        rows = chunk_rows(c)
        cum = _split_dot_lhs(tri, lw_ref[rows, :])
        yield
        lw = lw_ref[rows, :]
        r, k, v, a, b = (ref[rows, :].astype(F32) for ref in (r_ref, k_ref, v_ref, a_ref, b_ref))
        e_cum = jnp.exp(cum)
        e_inv = jnp.exp(-cum)
        e_end = jnp.exp(cum[C - 1:C, :] - cum)
        rt = r * e_cum
        at = (a * jnp.exp(cum - lw)).astype(BF16)
        kt = (k * e_inv).astype(BF16)
        bt = (b * e_inv).astype(BF16)
        kb = jnp.concatenate([stack(kt), stack(bt)], axis=0)
        ga = lax.dot_general(at, kb, _NT, preferred_element_type=F32)
        gr = lax.dot_general(rt.astype(BF16), kb, _NT, preferred_element_type=F32)
        ats_scr[c] = stack(at)
        vs_scr[c] = stack(v.astype(BF16))
        rt_scr[rows, :] = rt
        kend_scr[rows, :] = (k * e_end).astype(BF16)
        bend_scr[rows, :] = (b * e_end).astype(BF16)
        wend_scr[c] = jnp.broadcast_to(e_cum[C - 1:C, :], (8, LANES))
        yield
        a_ab = jnp.where(strict, ga[:, LANES:], 0.0)
        lp_scr[c] = a_ab.astype(BF16)
        t_scr[c] = eye_pair + a_ab
        aak_scr[c] = jnp.where(strict, ga[:, :LANES], 0.0).astype(BF16)
        ark_scr[c] = jnp.where(incl, gr[:, :LANES], 0.0).astype(BF16)
        arb_scr[c] = jnp.where(incl, gr[:, LANES:], 0.0).astype(BF16)

    def square_only(c):
        lp = lp_scr[c]
        lp2 = _dot(lp, stack(lp))
        yield
        lp_scr[c] = lp2.astype(BF16)

    def fold_and_square(c):
        lp = lp_scr[c]
        lp_bd = stack(lp)
        t = t_scr[c]
        tl = _dot(t.astype(BF16), lp_bd)
        lp2 = _dot(lp, lp_bd)
        yield
        t_scr[c] = t + tl
        lp_scr[c] = lp2.astype(BF16)

    def fold_only(c):
        t = t_scr[c]
        tl = _dot(t.astype(BF16), stack(lp_scr[c]))
        yield
        t_scr[c] = t + tl

    def chunk_maps(c):
        rows = chunk_rows(c)
        vs = vs_scr[c]
        x = _dot(aak_scr[c], vs)
        y0a = _dot(ark_scr[c], vs)
        yield
        au = _dot(t_scr[c].astype(BF16),
                  jnp.concatenate([ats_scr[c], stack(x.astype(BF16))], axis=1))
        yield
        ah = au[:, :LANES].astype(BF16)
        u0 = au[:, LANES:].astype(BF16)
        ry = _dot(arb_scr[c], jnp.concatenate([stack(ah), stack(u0)], axis=1))
        k_end = kend_scr[rows, :]
        b_end = bend_scr[rows, :]
        p = lax.dot_general(b_end, ah, _TN, preferred_element_type=F32)
        q = lax.dot_general(jnp.concatenate([k_end, b_end], axis=0),
                            jnp.concatenate([v_ref[rows, :], u0], axis=0), _TN,
                            preferred_element_type=F32)
        yield
        rh_scr[cur, rows, :] = rt_scr[rows, :] + ry[:, :LANES]
        y0_scr[cur, rows, :] = y0a + ry[:, LANES:]
        p = jnp.where(same, p, 0.0)
        p_pair = p[:C, :] + p[C:, :] + eye_pair * wend_scr[c][0:1, :]
        p_hi = p_pair.astype(BF16)
        p_scr[cur, c] = jnp.concatenate([p_hi, (p_pair - p_hi.astype(F32)).astype(BF16)], axis=1)
        q = jnp.where(same, q, 0.0)
        q_scr[cur, c] = q[:C, :] + q[C:, :]

    def scan_chunks():
        for c in range(n_chunks):
            rows = chunk_rows(c)
            st = st_scr[...]
            st_hi = st.astype(BF16)
            st_lo = (st - st_hi.astype(F32)).astype(BF16)
            hi_bd = stack(st_hi)
            pc = p_scr[prev, c]
            p_hi = pc[:, :LANES]
            st_new = _dot(p_hi, hi_bd) + (_dot(p_hi, stack(st_lo)) + _dot(pc[:, LANES:], hi_bd))
            ys = _dot(rh_scr[prev, rows, :].astype(BF16), hi_bd)
            yield
            y_ref[rows, :] = (y0_scr[prev, rows, :] + ys).astype(y_ref.dtype)
            st_scr[...] = st_new + q_scr[prev, c]

    stages = [gram_terms, square_only]
    n = 4
    while n < C:
        stages.append(fold_and_square)
        n *= 2
    stages += [fold_only, chunk_maps]

    @pl.when(step == 0)
    def _():
        rh_scr[1] = jnp.zeros(rh_scr.shape[1:], F32)
        y0_scr[1] = jnp.zeros(y0_scr.shape[1:], F32)
        p_scr[1] = jnp.zeros(p_scr.shape[1:], BF16)
        q_scr[1] = jnp.zeros(q_scr.shape[1:], F32)

    def chunk_pipeline(c):
        for stage in stages:
            yield from stage(c)

    st_scr[...] = jnp.zeros((C, LANES), F32)
    gens =[chunk_pipeline(c) for c in range(n_chunks)]
    scan = scan_chunks()
    live = set(range(n_chunks))
    t = 0
    scan_live = True
    while live or scan_live:
        for i in sorted(live):
            if t >= i and (t - i) % WKV_PHASE_GAP == 0:
                try:
                    next(gens[i])
                except StopIteration:
                    live.discard(i)
        if scan_live and t % WKV_SCAN_GAP == 0:
            try:
                next(scan)
            except StopIteration:
                scan_live = False
        t += 1


def _wkv(r, lw, k, v, an, bn):
    B, S, D = r.shape
    n_pairs = D // LANES
    n_steps = B * n_pairs
    n_chunks = S // CHUNK
    in_blk = pl.BlockSpec((None, S, LANES), lambda s: (jnp.minimum(s, n_steps - 1) // n_pairs, 0,
                                                       jnp.minimum(s, n_steps - 1) % n_pairs))
    out_blk = pl.BlockSpec((None, S, LANES), lambda s: (jnp.maximum(s - 1, 0) // n_pairs, 0,
                                                        jnp.maximum(s - 1, 0) % n_pairs))
    mat = lambda dt: pltpu.VMEM((n_chunks, LANES, LANES), dt)
    pair = lambda dt: pltpu.VMEM((n_chunks, CHUNK, LANES), dt)
    seq = lambda dt: pltpu.VMEM((S, LANES), dt)
    return pl.pallas_call(
        _wkv_kernel,
        grid=(n_steps + 1,),
        in_specs=[in_blk] * 6,
        out_specs=out_blk,
        out_shape=jax.ShapeDtypeStruct((B, S, D), BF16),
        scratch_shapes=[pair(BF16), pair(F32), pair(BF16), pair(BF16), pair(BF16), mat(BF16), mat(BF16),
                        seq(F32), seq(BF16), seq(BF16), pltpu.VMEM((n_chunks, 8, LANES), F32),
                        pltpu.VMEM((2, S, LANES), F32), pltpu.VMEM((2, S, LANES), F32),
                        pltpu.VMEM((2, n_chunks, CHUNK, 2 * LANES), BF16),
                        pltpu.VMEM((2, n_chunks, CHUNK, LANES), F32),
                        pltpu.VMEM((CHUNK, LANES), F32)],
        compiler_params=pltpu.CompilerParams(vmem_limit_bytes=VMEM_LIMIT,
                                             dimension_semantics=("arbitrary",)),
        name="wkv",
    )(r, lw, k, v, an, bn)


def _a_post_kernel(y_ref, r_ref, k_ref, v_ref, sg_ref, x_ref, mod_ref, lng_ref, lnb_ref, rk_ref,
                   wout_ref, kvg_ref, wkv_ref, kng_ref, cos_ref, sin_ref,
                   xr_out, ksh_out, vsh_out):
    D = x_ref.shape[-1]
    e = _head_ones()
    n_lane_groups = D // LANES
    f32 = lambda ref: ref[...].astype(F32)
    y = f32(y_ref)
    mean = _head_sum(y, e) * (1.0 / HEAD_DIM)
    d = y - mean
    var = _head_sum(d * d, e) * (1.0 / HEAD_DIM)
    yn = d * lax.rsqrt(var + GN_EPS) * lng_ref[...] + lnb_ref[...]
    bonus = _head_sum(f32(r_ref) * f32(k_ref) * rk_ref[...], e) * f32(v_ref)
    mix = _mm((yn + bonus) * f32(sg_ref), wout_ref[...])
    gate = mod_ref[...][:, 2 * D:]
    xr = x_ref[...] + gate * mix
    xr_out[...] = xr

    kv = _mm(_rms_rows(xr) * kvg_ref[...], wkv_ref[...])
    ks = kv[:, :D]
    ks = ks * lax.rsqrt(_head_sum(ks * ks, e) * (1.0 / HEAD_DIM) + NORM_EPS) * kng_ref[...]
    cos = _tile_lanes(cos_ref[...], n_lane_groups)
    sin = _tile_lanes(sin_ref[...], n_lane_groups)
    ksh_out[...] = (ks * cos + _rot_half(ks) * sin).astype(ksh_out.dtype)
    vsh_out[...] = kv[:, D:].astype(vsh_out.dtype)


def _a_post(y, r, k, v, sg, x, mod3, ln_g, ln_b, r_k, w_out, kv_g, w_kv, kn_g, cos_t, sin_t):
    B, S, D = x.shape
    ts = SEQ_TILE
    tile = pl.BlockSpec((None, ts, D), lambda b, i: (b, i, 0))
    vec = pl.BlockSpec((1, D), lambda b, i: (0, 0))
    rope = pl.BlockSpec((ts, LANES), lambda b, i: (i, 0))

    def full(a):
        return pl.BlockSpec(a.shape, lambda b, i: (0,) * a.ndim)

    act = lambda dt: jax.ShapeDtypeStruct((B, S, D), dt)
    return pl.pallas_call(
        _a_post_kernel,
        grid=(B, S // ts),
        in_specs=[tile] * 6 + [pl.BlockSpec((None, 1, 3 * D), lambda b, i: (b, 0, 0)), vec, vec, vec,
                               full(w_out), vec, full(w_kv), vec, rope, rope],
        out_specs=[tile] * 3,
        out_shape=[act(F32), act(BF16), act(BF16)],
        compiler_params=pltpu.CompilerParams(vmem_limit_bytes=VMEM_LIMIT),
        name="a_post",
    )(y, r, k, v, sg, x, mod3, ln_g, ln_b, r_k, w_out, kv_g, w_kv, kn_g, cos_t, sin_t)


def _b_pre_kernel(x_ref, mod_ref, ng_ref, win_ref, qg_ref, cos_ref, sin_ref, q_out, sg_out):
    D = x_ref.shape[-1]
    nq = q_out.shape[-1]
    e = _head_ones()
    mod = mod_ref[...]
    shift, scale = mod[:, :D], mod[:, D:2 * D]
    h = (_rms_rows(x_ref[...]) * (ng_ref[...] * (1.0 + scale)) + shift).astype(BF16)
    cos = _tile_lanes(cos_ref[...], D // LANES)
    sin = _tile_lanes(sin_ref[...], D // LANES)
    for g in range(nq // D):
        q = _dot(h, win_ref[:, g * D:(g + 1) * D])
        q = q * lax.rsqrt(_head_sum(q * q, e) * (1.0 / HEAD_DIM) + NORM_EPS) * qg_ref[...]
        q_out[:, g * D:(g + 1) * D] = ((q * cos + _rot_half(q) * sin) * Q_SCALE).astype(q_out.dtype)
    gate = _dot(h, win_ref[:, nq:])
    sg_out[...] = (gate * _sigmoid(gate)).astype(sg_out.dtype)


def _b_pre(xr, mod3, ng, w_in, qn_g, cos_t, sin_t):
    B, S, D = xr.shape
    nq = w_in.shape[1] - D
    ts = SEQ_TILE
    tile = pl.BlockSpec((None, ts, D), lambda b, i: (b, i, 0))
    vec = pl.BlockSpec((1, D), lambda b, i: (0, 0))
    rope = pl.BlockSpec((ts, LANES), lambda b, i: (i, 0))
    return pl.pallas_call(
        _b_pre_kernel,
        grid=(B, S // ts),
        in_specs=[tile, pl.BlockSpec((None, 1, 3 * D), lambda b, i: (b, 0, 0)), vec,
                  pl.BlockSpec(w_in.shape, lambda b, i: (0, 0)), vec, rope, rope],
        out_specs=[pl.BlockSpec((None, ts, nq), lambda b, i: (b, i, 0)), tile],
        out_shape=[jax.ShapeDtypeStruct((B, S, nq), BF16), jax.ShapeDtypeStruct((B, S, D), BF16)],
        compiler_params=pltpu.CompilerParams(vmem_limit_bytes=VMEM_LIMIT),
        name="b_pre",
    )(xr, mod3, ng, w_in, qn_g, cos_t, sin_t)


def _attn_kernel(q0_ref, q1_ref, q2_ref, k_ref, v_ref, sg_ref, o_ref,
                 tmp_scr, qr_scr, kr_scr, vr_scr, acc_scr, m_scr, l_scr, s_scr, p_scr):
    S = k_ref.shape[0]
    blk = BAND_BLOCK
    lane = lax.broadcasted_iota(jnp.int32, (1, LANES), 1)
    head0 = lane < HEAD_DIM

    def head_ones(nk):
        hsel = (lax.broadcasted_iota(jnp.int32, (HEADS_PER_GROUP * nk, LANES), 1) // HEAD_DIM
                == lax.broadcasted_iota(jnp.int32, (HEADS_PER_GROUP * nk, LANES), 0) // nk)
        return hsel.astype(F32).astype(BF16)

    def upcast(src_ref):
        rows_per = 4 * blk

        def tile(t, carry):
            rows = pl.ds(pl.multiple_of(t * rows_per, rows_per), rows_per)
            tmp_scr[rows, :] = src_ref[rows, :].astype(F32)
            return carry
        lax.fori_loop(0, S // rows_per, tile, 0)

    for gi, (_, dil) in enumerate(DIL_GROUPS):
        if dil == 1:
            continue
        seg = S // dil
        for src_ref, dst_scr in ((q1_ref if gi == 1 else q2_ref, qr_scr), (k_ref, kr_scr), (v_ref, vr_scr)):
            upcast(src_ref)
            for rho in range(dil):
                dst_scr[gi - 1, rho * seg:(rho + 1) * seg, :] = (
                    tmp_scr[pl.ds(rho, seg, stride=dil), :].astype(BF16))

    def geometry(gi, rho, n):
        dil = DIL_GROUPS[gi][1]
        j = rho * (S // dil // blk) + n
        lo = (j - 1) * blk if n > 0 else j * blk
        nk = (j + 1) * blk - lo
        if gi == 0:
            srcs = (q0_ref, k_ref, v_ref)
        else:
            srcs = (qr_scr.at[gi - 1], kr_scr.at[gi - 1], vr_scr.at[gi - 1])
        return j, lo, nk, srcs

    def token_rows(gi, rho, n, start, size):
        dil = DIL_GROUPS[gi][1]
        first = rho + dil * (blk * n + start)
        return slice(first, first + size) if dil == 1 else pl.ds(first, size, stride=dil)

    def per_head(x, other):
        return jnp.concatenate([jnp.where(head0, x, other), jnp.where(head0, other, x)], axis=0)

    def scores(b, slot):
        j, lo, nk, (q_src, k_src, _) = geometry(*b)
        kc = k_src[lo:lo + nk, :]
        s_scr[slot, :, :2 * nk] = lax.dot_general(q_src[j * blk:(j + 1) * blk, :],
                                                  per_head(kc, jnp.zeros_like(kc)), _NT,
                                                  preferred_element_type=F32)

    def softmax(b, slot):
        gi = b[0]
        win, dil = DIL_GROUPS[gi]
        nk = geometry(*b)[2]
        rt = blk // 2
        qi = lax.broadcasted_iota(jnp.int32, (rt, nk), 0)
        kj = lax.broadcasted_iota(jnp.int32, (rt, nk), 1)
        for t in range(blk // rt):
            diff = (nk - blk) + (qi + t * rt) - kj
            valid = (diff >= 0) & (diff <= win // dil)
            ms = []
            for h in range(HEADS_PER_GROUP):
                s = jnp.where(valid, s_scr[slot, t * rt:(t + 1) * rt, h * nk:(h + 1) * nk], NEG_INF)
                m = jnp.max(s, axis=-1, keepdims=True)
                p_scr[slot, t * rt:(t + 1) * rt, h * nk:(h + 1) * nk] = jnp.exp2(s - m).astype(BF16)
                ms.append(m)
            m_scr[gi, token_rows(*b, t * rt, rt), :] = jnp.where(head0, ms[0], ms[1])

    def weighted_values(b, slot):
        gi = b[0]
        _, lo, nk, (_, _, v_src) = geometry(*b)
        vc = v_src[lo:lo + nk, :]
        zero = jnp.zeros_like(vc)
        v2 = jnp.concatenate([per_head(vc, zero), head_ones(nk)], axis=1)
        pvl = _dot(p_scr[slot, :, :2 * nk], v2)
        rows = token_rows(*b, 0, blk)
        acc_scr[gi, rows, :] = pvl[:, :LANES]
        l_scr[gi, rows, :] = pvl[:, LANES:]

    blocks = [(gi, rho, n) for gi, (_, dil) in enumerate(DIL_GROUPS)
              for rho in range(dil) for n in range(S // dil // blk)]
    gap = ATTN_PHASE_GAP
    n_slots = s_scr.shape[0]
    assert n_slots > 2 * gap
    for t in range(len(blocks) + 2 * gap):
        if 0 <= t - 2 * gap < len(blocks):
            weighted_values(blocks[t - 2 * gap], (t - 2 * gap) % n_slots)
        if t < len(blocks):
            scores(blocks[t], t % n_slots)
        if 0 <= t - gap < len(blocks):
            softmax(blocks[t - gap], (t - gap) % n_slots)

    def merge(t, carry):
        rows = pl.ds(pl.multiple_of(t * blk, blk), blk)
        ms = [m_scr[g, rows, :] for g in range(len(DIL_GROUPS))]
        m_all = functools.reduce(jnp.maximum, ms)
        ws = [jnp.exp2(m - m_all) for m in ms]
        num = functools.reduce(jnp.add, [w * acc_scr[g, rows, :] for g, w in enumerate(ws)])
        den = functools.reduce(jnp.add, [w * l_scr[g, rows, :] for g, w in enumerate(ws)])
        o_ref[rows, :] = (num / den * sg_ref[rows, :].astype(F32)).astype(o_ref.dtype)
        return carry

    lax.fori_loop(0, S // blk, merge, 0)


def _attn(q, k_sh, v_sh, sg):
    B, S, D = k_sh.shape
    n_groups = len(DIL_GROUPS)
    n_pairs = D // LANES
    qspec = lambda g: pl.BlockSpec((None, S, LANES), lambda b, p: (b, 0, g * n_pairs + p))
    blk = pl.BlockSpec((None, S, LANES), lambda b, p: (b, 0, p))
    res = lambda: pltpu.VMEM((n_groups - 1, S, LANES), BF16)
    nat = lambda: pltpu.VMEM((n_groups, S, LANES), F32)
    return pl.pallas_call(
        _attn_kernel,
        grid=(B, n_pairs),
        in_specs=[qspec(0), qspec(1), qspec(2), blk, blk, blk],
        out_specs=blk,
        out_shape=jax.ShapeDtypeStruct((B, S, D), BF16),
        scratch_shapes=[pltpu.VMEM((S, LANES), F32), res(), res(), res(), nat(), nat(), nat(),
                        pltpu.VMEM((ATTN_SLOTS, BAND_BLOCK, 2 * HEADS_PER_GROUP * BAND_BLOCK), F32),
                        pltpu.VMEM((ATTN_SLOTS, BAND_BLOCK, 2 * HEADS_PER_GROUP * BAND_BLOCK), BF16)],
        compiler_params=pltpu.CompilerParams(vmem_limit_bytes=VMEM_LIMIT),
        name="attn",
    )(q, q, q, k_sh, v_sh, sg)


def _b_post_kernel(att_ref, x_ref, mod_ref, wout_ref, out_ref):
    D = x_ref.shape[-1]
    gate = mod_ref[...][:, 2 * D:]
    out_ref[...] = x_ref[...] + gate * _dot(att_ref[...], wout_ref[...])


def _b_post(att, xr, mod3, w_out):
    B, S, D = xr.shape
    ts = SEQ_TILE
    tile = pl.BlockSpec((None, ts, D), lambda b, i: (b, i, 0))
    return pl.pallas_call(
        _b_post_kernel,
        grid=(B, S // ts),
        in_specs=[tile, tile, pl.BlockSpec((None, 1, 3 * D), lambda b, i: (b, 0, 0)),
                  pl.BlockSpec(w_out.shape, lambda b, i: (0, 0))],
        out_specs=tile,
        out_shape=jax.ShapeDtypeStruct((B, S, D), F32),
        compiler_params=pltpu.CompilerParams(vmem_limit_bytes=VMEM_LIMIT),
        name="b_post",
    )(att, xr, mod3, w_out)


def _rope_tables(seq):
    pos = jnp.arange(seq, dtype=F32)
    inv = ROPE_THETA ** (-jnp.arange(0, HEAD_DIM, 2, dtype=F32) / HEAD_DIM)
    ang = pos[:, None] * inv[None, :]
    cos, sin = jnp.cos(ang), jnp.sin(ang)
    cos_t = jnp.concatenate([cos, cos] * HEADS_PER_GROUP, axis=-1)
    sin_t = jnp.concatenate([-sin, sin] * HEADS_PER_GROUP, axis=-1)
    return cos_t, sin_t


def kernel(x, c, a_ada_w, a_ada_b, a_norm_g, a_mix_mu, a_w_in, a_w0, a_w1, a_w2, a_a0, a_a1, a_a2,
           a_k_k, a_k_a, a_r_k, a_ln_g, a_ln_b, a_w_out, kv_norm_g, w_kv, k_norm_g,
           b_ada_w, b_ada_b, b_norm_g, b_w_in, b_q_norm_g, b_w_out):
    B, S, D = x.shape
    assert a_ada_w.shape[0] == 1 and b_ada_w.shape[0] == 1
    assert D % LANES == 0 and S % (BAND_BLOCK * DIL_GROUPS[-1][1]) == 0 and S % SEQ_TILE == 0
    assert all(win // dil == BAND_BLOCK for win, dil in DIL_GROUPS)
    assert b_w_in.shape[-1] == (len(DIL_GROUPS) + 1) * D
    row = lambda t: t.reshape(1, -1)
    per_head = lambda t: jnp.tile(t.reshape(1, HEAD_DIM), (1, D // HEAD_DIM))

    mod_a, mod_b = _adaln(c, a_ada_w, a_ada_b, b_ada_w, b_ada_b)
    mod_a = mod_a.reshape(B, 1, 3 * D)
    mod_b = mod_b.reshape(B, 1, 3 * D)
    cos_t, sin_t = _rope_tables(S)

    r, lw, k, v, an, bn, sg_a = _a_pre(
        x, mod_a, a_norm_g, a_mix_mu[0], a_w_in[0].astype(BF16), a_w0, a_w1[0].astype(BF16),
        a_w2[0].astype(BF16), a_a0, a_a1[0].astype(BF16), a_a2[0].astype(BF16), a_k_k, a_k_a)
    y = _wkv(r, lw, k, v, an, bn)
    xr, k_sh, v_sh = _a_post(
        y, r, k, v, sg_a, x, mod_a, a_ln_g, a_ln_b, row(a_r_k[0]), a_w_out[0].astype(BF16),
        row(kv_norm_g), w_kv.astype(BF16), per_head(k_norm_g), cos_t, sin_t)

    q, sg_b = _b_pre(xr, mod_b, b_norm_g, b_w_in[0].astype(BF16), per_head(b_q_norm_g[0]), cos_t, sin_t)
    att = _attn(q, k_sh, v_sh, sg_b)
    return _b_post(att, xr, mod_b, b_w_out[0].astype(BF16))
```

```python
import functools

import jax
import jax.numpy as jnp
from jax import lax
from jax.experimental import pallas as pl
from jax.experimental.pallas import tpu as pltpu

F32 = jnp.float32
BF16 = jnp.bfloat16

HEAD_DIM = 64
LANES = 128
HEADS_PER_GROUP = LANES // HEAD_DIM
DIL_GROUPS = ((128, 1), (512, 4), (2048, 16))
BAND_BLOCK = 128
ROPE_THETA = 10000.0
NORM_EPS = 1e-6
GN_EPS = 64e-5
NEG_INF = -1e30
DECAY_SCALE = 0.6065306597126334
CHUNK = 64
WKV_PHASE_GAP = 4
WKV_SCAN_GAP = 2
ATTN_PHASE_GAP = 2
ATTN_SLOTS = 8
SEQ_TILE = 512
Q_SCALE = HEAD_DIM ** -0.5 * 1.4426950408889634
VMEM_LIMIT = 56 * 1024 * 1024

_NT = (((1,), (1,)), ((), ()))
_TN = (((0,), (0,)), ((), ()))


def _dot(a, b):
    return jnp.dot(a, b, preferred_element_type=F32)


def _mm(a, b):
    return _dot(a.astype(BF16), b.astype(BF16))


def _split_dot_lhs(e, x):
    hi = x.astype(BF16)
    lo = (x - hi.astype(F32)).astype(BF16)
    return _dot(e, hi) + _dot(e, lo)


def _mm_3pass(a, b):
    a_hi = a.astype(BF16)
    a_lo = (a - a_hi.astype(F32)).astype(BF16)
    b_hi = b.astype(BF16)
    b_lo = (b - b_hi.astype(F32)).astype(BF16)
    return _dot(a_hi, b_hi) + (_dot(a_hi, b_lo) + _dot(a_lo, b_hi))


def _head_sum(x, e):
    parts = [_dot(x[:, g * LANES:(g + 1) * LANES].astype(BF16), e) for g in range(x.shape[1] // LANES)]
    return parts[0] if len(parts) == 1 else jnp.concatenate(parts, axis=1)


def _rot_half(x):
    lane = lax.broadcasted_iota(jnp.int32, (1, LANES), 1)
    first = (lane % HEAD_DIM) < (HEAD_DIM // 2)
    parts = []
    for g in range(x.shape[1] // LANES):
        xg = x[:, g * LANES:(g + 1) * LANES]
        up = pltpu.roll(xg, LANES - HEAD_DIM // 2, axis=1)
        dn = pltpu.roll(xg, HEAD_DIM // 2, axis=1)
        parts.append(jnp.where(first, up, dn))
    return parts[0] if len(parts) == 1 else jnp.concatenate(parts, axis=1)


def _tile_lanes(t, n):
    return t if n == 1 else jnp.concatenate([t] * n, axis=1)


def _rms_rows(x):
    return x * lax.rsqrt(jnp.mean(x * x, axis=-1, keepdims=True) + NORM_EPS)


def _sigmoid(x):
    return 1.0 / (1.0 + jnp.exp(-x))


def _head_ones():
    r = lax.broadcasted_iota(jnp.int32, (LANES, LANES), 0) // HEAD_DIM
    c = lax.broadcasted_iota(jnp.int32, (LANES, LANES), 1) // HEAD_DIM
    return (r == c).astype(BF16)


def _adaln_kernel(c_ref, wa_ref, ba_ref, wb_ref, bb_ref, oa_ref, ob_ref):
    c = c_ref[...]
    sc = c * _sigmoid(c)
    oa_ref[...] = _mm_3pass(sc, wa_ref[...]) + ba_ref[...]
    ob_ref[...] = _mm_3pass(sc, wb_ref[...]) + bb_ref[...]


def _adaln(c, wa, ba, wb, bb):
    B, D = c.shape
    n3 = wa.shape[-1]
    tn = 512
    wspec = pl.BlockSpec((None, D, tn), lambda j: (0, 0, j))
    bspec = pl.BlockSpec((1, tn), lambda j: (0, j))
    ospec = pl.BlockSpec((B, tn), lambda j: (0, j))
    return pl.pallas_call(
        _adaln_kernel,
        grid=(n3 // tn,),
        in_specs=[pl.BlockSpec((B, D), lambda j: (0, 0)), wspec, bspec, wspec, bspec],
        out_specs=[ospec, ospec],
        out_shape=[jax.ShapeDtypeStruct((B, n3), F32)] * 2,
        name="adaln",
    )(c, wa, ba, wb, bb)


def _a_pre_kernel(x_ref, xp_ref, mod_ref, ng_ref, mu_ref, win_ref, w0_ref, w1_ref, w2_ref,
                  a0_ref, a1_ref, a2_ref, kk_ref, ka_ref,
                  r_out, lw_out, k_out, v_out, an_out, bn_out, sg_out):
    D = x_ref.shape[-1]
    i = pl.program_id(1)
    mod = mod_ref[...]
    shift, scale = mod[:, :D], mod[:, D:2 * D]
    gain = ng_ref[...] * (1.0 + scale)

    h = _rms_rows(x_ref[...]) * gain + shift
    hp = _rms_rows(xp_ref[...]) * gain + shift
    hp = jnp.where(i == 0, 0.0, hp[7:8, :])
    row = lax.broadcasted_iota(jnp.int32, h.shape, 0)
    hs = jnp.where(row == 0, hp, pltpu.roll(h, 1, axis=0))
    xx = hs - h
    mu = mu_ref[...]
    lerp = lambda p: (h + xx * mu[p:p + 1, :]).astype(BF16)
    wl = _mm(jnp.tanh(_mm(lerp(4), w1_ref[...])), w2_ref[...])
    al = _mm(_mm(lerp(5), a1_ref[...]), a2_ref[...])
    r = _dot(lerp(0), win_ref[:, 0 * D:1 * D])
    k = _dot(lerp(1), win_ref[:, 1 * D:2 * D])
    v = _dot(lerp(2), win_ref[:, 2 * D:3 * D])
    g = _dot(lerp(3), win_ref[:, 3 * D:4 * D])

    a = _sigmoid(a0_ref[...] + al)

    kk = k * kk_ref[...]
    ss = _head_sum(kk * kk, _head_ones())
    kk = kk * lax.rsqrt(jnp.maximum(ss, 1e-24))

    r_out[...] = r.astype(r_out.dtype)
    lw_out[...] = (-DECAY_SCALE) * _sigmoid(w0_ref[...] + wl)
    k_out[...] = (k * (1.0 + (a - 1.0) * ka_ref[...])).astype(k_out.dtype)
    v_out[...] = v.astype(v_out.dtype)
    an_out[...] = (-kk).astype(an_out.dtype)
    bn_out[...] = (kk * a).astype(bn_out.dtype)
    sg_out[...] = (g * _sigmoid(g)).astype(sg_out.dtype)


def _a_pre(x, mod3, ng, mu, w_in, w0, w1, w2, a0, a1, a2, k_k, k_a):
    B, S, D = x.shape
    ts = SEQ_TILE
    tile = pl.BlockSpec((None, ts, D), lambda b, i: (b, i, 0))
    prev = pl.BlockSpec((None, 8, D), lambda b, i: (b, jnp.maximum(i * (ts // 8) - 1, 0), 0))
    vec = pl.BlockSpec((1, D), lambda b, i: (0, 0))

    def full(a):
        return pl.BlockSpec(a.shape, lambda b, i: (0,) * a.ndim, pipeline_mode=pl.Buffered(1))

    act = lambda dt: jax.ShapeDtypeStruct((B, S, D), dt)
    return pl.pallas_call(
        _a_pre_kernel,
        grid=(B, S // ts),
        in_specs=[tile, prev, pl.BlockSpec((None, 1, 3 * D), lambda b, i: (b, 0, 0)), vec, full(mu),
                  full(w_in), vec, full(w1), full(w2), vec, full(a1), full(a2), vec, vec],
        out_specs=[tile] * 7,
        out_shape=[act(BF16), act(F32), act(BF16), act(BF16), act(BF16), act(BF16), act(BF16)],
        compiler_params=pltpu.CompilerParams(vmem_limit_bytes=VMEM_LIMIT),
        name="a_pre",
    )(x, x, mod3, ng, mu, w_in, w0, w1, w2, a0, a1, a2, k_k, k_a)


def _wkv_kernel(r_ref, lw_ref, k_ref, v_ref, a_ref, b_ref, y_ref,
                lp_scr, t_scr, aak_scr, ark_scr, arb_scr, ats_scr, vs_scr, rt_scr, kend_scr, bend_scr,
                wend_scr, rh_scr, y0_scr, p_scr, q_scr, st_scr):
    C = CHUNK
    n_chunks = r_ref.shape[0] // C
    assert C == HEAD_DIM
    step = pl.program_id(0)
    cur = step % 2
    prev = 1 - cur

    lane = lax.broadcasted_iota(jnp.int32, (1, LANES), 1)
    head0 = lane < HEAD_DIM
    ti = lax.broadcasted_iota(jnp.int32, (C, LANES), 0)
    ii = lax.broadcasted_iota(jnp.int32, (C, LANES), 1) % C
    strict = ii < ti
    incl = ii <= ti
    eye_pair = (ii == ti).astype(F32)
    ri = lax.broadcasted_iota(jnp.int32, (LANES, LANES), 0)
    ci = lax.broadcasted_iota(jnp.int32, (LANES, LANES), 1)
    same = (ri // HEAD_DIM) == (ci // HEAD_DIM)
    tri = (lax.broadcasted_iota(jnp.int32, (C, C), 1)
           <= lax.broadcasted_iota(jnp.int32, (C, C), 0)).astype(BF16)

    def stack(x):
        zero = jnp.zeros_like(x)
        return jnp.concatenate([jnp.where(head0, x, zero), jnp.where(head0, zero, x)], axis=0)

    def chunk_rows(c):
        return slice(c * C, (c + 1) * C)

    def gram_terms(c):
        rows = chunk_rows(c)
        cum = _split_dot_lhs(tri, lw_ref[rows, :])
        yield
        lw = lw_ref[rows, :]
        r, k, v, a, b = (ref[rows, :].astype(F32) for ref in (r_ref, k_ref, v_ref, a_ref, b_ref))
        e_cum = jnp.exp(cum)
        e_inv = jnp.exp(-cum)
        e_end = jnp.exp(cum[C - 1:C, :] - cum)
        rt = r * e_cum
        at = (a * jnp.exp(cum - lw)).astype(BF16)
        kt = (k * e_inv).astype(BF16)
        bt = (b * e_inv).astype(BF16)
        kb = jnp.concatenate([stack(kt), stack(bt)], axis=0)
        ga = lax.dot_general(at, kb, _NT, preferred_element_type=F32)
        gr = lax.dot_general(rt.astype(BF16), kb, _NT, preferred_element_type=F32)
        ats_scr[c] = stack(at)
        vs_scr[c] = stack(v.astype(BF16))
        rt_scr[rows, :] = rt
        kend_scr[rows, :] = (k * e_end).astype(BF16)
        bend_scr[rows, :] = (b * e_end).astype(BF16)
        wend_scr[c] = jnp.broadcast_to(e_cum[C - 1:C, :], (8, LANES))
        yield
        a_ab = jnp.where(strict, ga[:, LANES:], 0.0)
        lp_scr[c] = a_ab.astype(BF16)
        t_scr[c] = eye_pair + a_ab
        aak_scr[c] = jnp.where(strict, ga[:, :LANES], 0.0).astype(BF16)
        ark_scr[c] = jnp.where(incl, gr[:, :LANES], 0.0).astype(BF16)
        arb_scr[c] = jnp.where(incl, gr[:, LANES:], 0.0).astype(BF16)

    def square_only(c):
        lp = lp_scr[c]
        lp2 = _dot(lp, stack(lp))
        yield
        lp_scr[c] = lp2.astype(BF16)

    def fold_and_square(c):
        lp = lp_scr[c]
        lp_bd = stack(lp)
        t = t_scr[c]
        tl = _dot(t.astype(BF16), lp_bd)
        lp2 = _dot(lp, lp_bd)
        yield
        t_scr[c] = t + tl
        lp_scr[c] = lp2.astype(BF16)

    def fold_only(c):
        t = t_scr[c]
        tl = _dot(t.astype(BF16), stack(lp_scr[c]))
        yield
        t_scr[c] = t + tl

    def chunk_maps(c):
        rows = chunk_rows(c)
        vs = vs_scr[c]
        x = _dot(aak_scr[c], vs)
        y0a = _dot(ark_scr[c], vs)
        yield
        au = _dot(t_scr[c].astype(BF16),
                  jnp.concatenate([ats_scr[c], stack(x.astype(BF16))], axis=1))
        yield
        ah = au[:, :LANES].astype(BF16)
        u0 = au[:, LANES:].astype(BF16)
        ry = _dot(arb_scr[c], jnp.concatenate([stack(ah), stack(u0)], axis=1))
        k_end = kend_scr[rows, :]
        b_end = bend_scr[rows, :]
        p = lax.dot_general(b_end, ah, _TN, preferred_element_type=F32)
        q = lax.dot_general(jnp.concatenate([k_end, b_end], axis=0),
                            jnp.concatenate([v_ref[rows, :], u0], axis=0), _TN,
                            preferred_element_type=F32)
        yield
        rh_scr[cur, rows, :] = rt_scr[rows, :] + ry[:, :LANES]
        y0_scr[cur, rows, :] = y0a + ry[:, LANES:]
        p = jnp.where(same, p, 0.0)
        p_pair = p[:C, :] + p[C:, :] + eye_pair * wend_scr[c][0:1, :]
        p_hi = p_pair.astype(BF16)
        p_scr[cur, c] = jnp.concatenate([p_hi, (p_pair - p_hi.astype(F32)).astype(BF16)], axis=1)
        q = jnp.where(same, q, 0.0)
        q_scr[cur, c] = q[:C, :] + q[C:, :]

    def scan_chunks():
        for c in range(n_chunks):
            rows = chunk_rows(c)
            st = st_scr[...]
            st_hi = st.astype(BF16)
            st_lo = (st - st_hi.astype(F32)).astype(BF16)
            hi_bd = stack(st_hi)
            pc = p_scr[prev, c]
            p_hi = pc[:, :LANES]
            st_new = _dot(p_hi, hi_bd) + (_dot(p_hi, stack(st_lo)) + _dot(pc[:, LANES:], hi_bd))
            ys = _dot(rh_scr[prev, rows, :].astype(BF16), hi_bd)
            yield
            y_ref[rows, :] = (y0_scr[prev, rows, :] + ys).astype(y_ref.dtype)
            st_scr[...] = st_new + q_scr[prev, c]

    stages = [gram_terms, square_only]
    n = 4
    while n < C:
        stages.append(fold_and_square)
        n *= 2
    stages += [fold_only, chunk_maps]

    @pl.when(step == 0)
    def _():
        rh_scr[1] = jnp.zeros(rh_scr.shape[1:], F32)
        y0_scr[1] = jnp.zeros(y0_scr.shape[1:], F32)
        p_scr[1] = jnp.zeros(p_scr.shape[1:], BF16)
        q_scr[1] = jnp.zeros(q_scr.shape[1:], F32)

    def chunk_pipeline(c):
        for stage in stages:
            yield from stage(c)

    st_scr[...] = jnp.zeros((C, LANES), F32)
    gens =[chunk_pipeline(c) for c in range(n_chunks)]
    scan = scan_chunks()
    live = set(range(n_chunks))
    t = 0
    scan_live = True
    while live or scan_live:
        for i in sorted(live):
            if t >= i and (t - i) % WKV_PHASE_GAP == 0:
                try:
                    next(gens[i])
                except StopIteration:
                    live.discard(i)
        if scan_live and t % WKV_SCAN_GAP == 0:
            try:
                next(scan)
            except StopIteration:
                scan_live = False
        t += 1


def _wkv(r, lw, k, v, an, bn):
    B, S, D = r.shape
    n_pairs = D // LANES
    n_steps = B * n_pairs
    n_chunks = S // CHUNK
    in_blk = pl.BlockSpec((None, S, LANES), lambda s: (jnp.minimum(s, n_steps - 1) // n_pairs, 0,
                                                       jnp.minimum(s, n_steps - 1) % n_pairs))
    out_blk = pl.BlockSpec((None, S, LANES), lambda s: (jnp.maximum(s - 1, 0) // n_pairs, 0,
                                                        jnp.maximum(s - 1, 0) % n_pairs))
    mat = lambda dt: pltpu.VMEM((n_chunks, LANES, LANES), dt)
    pair = lambda dt: pltpu.VMEM((n_chunks, CHUNK, LANES), dt)
    seq = lambda dt: pltpu.VMEM((S, LANES), dt)
    return pl.pallas_call(
        _wkv_kernel,
        grid=(n_steps + 1,),
        in_specs=[in_blk] * 6,
        out_specs=out_blk,
        out_shape=jax.ShapeDtypeStruct((B, S, D), BF16),
        scratch_shapes=[pair(BF16), pair(F32), pair(BF16), pair(BF16), pair(BF16), mat(BF16), mat(BF16),
                        seq(F32), seq(BF16), seq(BF16), pltpu.VMEM((n_chunks, 8, LANES), F32),
                        pltpu.VMEM((2, S, LANES), F32), pltpu.VMEM((2, S, LANES), F32),
                        pltpu.VMEM((2, n_chunks, CHUNK, 2 * LANES), BF16),
                        pltpu.VMEM((2, n_chunks, CHUNK, LANES), F32),
                        pltpu.VMEM((CHUNK, LANES), F32)],
        compiler_params=pltpu.CompilerParams(vmem_limit_bytes=VMEM_LIMIT,
                                             dimension_semantics=("arbitrary",)),
        name="wkv",
    )(r, lw, k, v, an, bn)


def _a_post_kernel(y_ref, r_ref, k_ref, v_ref, sg_ref, x_ref, mod_ref, lng_ref, lnb_ref, rk_ref,
                   wout_ref, kvg_ref, wkv_ref, kng_ref, cos_ref, sin_ref,
                   xr_out, ksh_out, vsh_out):
    D = x_ref.shape[-1]
    e = _head_ones()
    n_lane_groups = D // LANES
    f32 = lambda ref: ref[...].astype(F32)
    y = f32(y_ref)
    mean = _head_sum(y, e) * (1.0 / HEAD_DIM)
    d = y - mean
    var = _head_sum(d * d, e) * (1.0 / HEAD_DIM)
    yn = d * lax.rsqrt(var + GN_EPS) * lng_ref[...] + lnb_ref[...]
    bonus = _head_sum(f32(r_ref) * f32(k_ref) * rk_ref[...], e) * f32(v_ref)
    mix = _mm((yn + bonus) * f32(sg_ref), wout_ref[...])
    gate = mod_ref[...][:, 2 * D:]
    xr = x_ref[...] + gate * mix
    xr_out[...] = xr

    kv = _mm(_rms_rows(xr) * kvg_ref[...], wkv_ref[...])
    ks = kv[:, :D]
    ks = ks * lax.rsqrt(_head_sum(ks * ks, e) * (1.0 / HEAD_DIM) + NORM_EPS) * kng_ref[...]
    cos = _tile_lanes(cos_ref[...], n_lane_groups)
    sin = _tile_lanes(sin_ref[...], n_lane_groups)
    ksh_out[...] = (ks * cos + _rot_half(ks) * sin).astype(ksh_out.dtype)
    vsh_out[...] = kv[:, D:].astype(vsh_out.dtype)


def _a_post(y, r, k, v, sg, x, mod3, ln_g, ln_b, r_k, w_out, kv_g, w_kv, kn_g, cos_t, sin_t):
    B, S, D = x.shape
    ts = SEQ_TILE
    tile = pl.BlockSpec((None, ts, D), lambda b, i: (b, i, 0))
    vec = pl.BlockSpec((1, D), lambda b, i: (0, 0))
    rope = pl.BlockSpec((ts, LANES), lambda b, i: (i, 0))

    def full(a):
        return pl.BlockSpec(a.shape, lambda b, i: (0,) * a.ndim)

    act = lambda dt: jax.ShapeDtypeStruct((B, S, D), dt)
    return pl.pallas_call(
        _a_post_kernel,
        grid=(B, S // ts),
        in_specs=[tile] * 6 + [pl.BlockSpec((None, 1, 3 * D), lambda b, i: (b, 0, 0)), vec, vec, vec,
                               full(w_out), vec, full(w_kv), vec, rope, rope],
        out_specs=[tile] * 3,
        out_shape=[act(F32), act(BF16), act(BF16)],
        compiler_params=pltpu.CompilerParams(vmem_limit_bytes=VMEM_LIMIT),
        name="a_post",
    )(y, r, k, v, sg, x, mod3, ln_g, ln_b, r_k, w_out, kv_g, w_kv, kn_g, cos_t, sin_t)


def _b_pre_kernel(x_ref, mod_ref, ng_ref, win_ref, qg_ref, cos_ref, sin_ref, q_out, sg_out):
    D = x_ref.shape[-1]
    nq = q_out.shape[-1]
    e = _head_ones()
    mod = mod_ref[...]
    shift, scale = mod[:, :D], mod[:, D:2 * D]
    h = (_rms_rows(x_ref[...]) * (ng_ref[...] * (1.0 + scale)) + shift).astype(BF16)
    cos = _tile_lanes(cos_ref[...], D // LANES)
    sin = _tile_lanes(sin_ref[...], D // LANES)
    for g in range(nq // D):
        q = _dot(h, win_ref[:, g * D:(g + 1) * D])
        q = q * lax.rsqrt(_head_sum(q * q, e) * (1.0 / HEAD_DIM) + NORM_EPS) * qg_ref[...]
        q_out[:, g * D:(g + 1) * D] = ((q * cos + _rot_half(q) * sin) * Q_SCALE).astype(q_out.dtype)
    gate = _dot(h, win_ref[:, nq:])
    sg_out[...] = (gate * _sigmoid(gate)).astype(sg_out.dtype)


def _b_pre(xr, mod3, ng, w_in, qn_g, cos_t, sin_t):
    B, S, D = xr.shape
    nq = w_in.shape[1] - D
    ts = SEQ_TILE
    tile = pl.BlockSpec((None, ts, D), lambda b, i: (b, i, 0))
    vec = pl.BlockSpec((1, D), lambda b, i: (0, 0))
    rope = pl.BlockSpec((ts, LANES), lambda b, i: (i, 0))
    return pl.pallas_call(
        _b_pre_kernel,
        grid=(B, S // ts),
        in_specs=[tile, pl.BlockSpec((None, 1, 3 * D), lambda b, i: (b, 0, 0)), vec,
                  pl.BlockSpec(w_in.shape, lambda b, i: (0, 0)), vec, rope, rope],
        out_specs=[pl.BlockSpec((None, ts, nq), lambda b, i: (b, i, 0)), tile],
        out_shape=[jax.ShapeDtypeStruct((B, S, nq), BF16), jax.ShapeDtypeStruct((B, S, D), BF16)],
        compiler_params=pltpu.CompilerParams(vmem_limit_bytes=VMEM_LIMIT),
        name="b_pre",
    )(xr, mod3, ng, w_in, qn_g, cos_t, sin_t)


def _attn_kernel(q0_ref, q1_ref, q2_ref, k_ref, v_ref, sg_ref, o_ref,
                 tmp_scr, tmp2_scr, qr_scr, kr_scr, vr_scr, acc_scr, m_scr, l_scr, s_scr, p_scr):
    S = k_ref.shape[0]
    blk = BAND_BLOCK
    n_groups = len(DIL_GROUPS)
    lane = lax.broadcasted_iota(jnp.int32, (1, LANES), 1)
    head0 = lane < HEAD_DIM

    def head_ones(nk):
        hsel = (lax.broadcasted_iota(jnp.int32, (HEADS_PER_GROUP * nk, LANES), 1) // HEAD_DIM
                == lax.broadcasted_iota(jnp.int32, (HEADS_PER_GROUP * nk, LANES), 0) // nk)
        return hsel.astype(F32).astype(BF16)

    def upcast(src_ref):
        rows_per = 4 * blk

        def tile(t, carry):
            rows = pl.ds(pl.multiple_of(t * rows_per, rows_per), rows_per)
            tmp_scr[rows, :] = src_ref[rows, :].astype(F32)
            return carry
        lax.fori_loop(0, S // rows_per, tile, 0)

    (_, d1), (_, d2) = DIL_GROUPS[1], DIL_GROUPS[2]
    assert d2 == d1 * d1 and DIL_GROUPS[0][1] == 1
    seg1, seg2 = S // d1, S // d2
    for src_ref, dst_scr, slots in ((q1_ref, qr_scr, (0,)), (q2_ref, qr_scr, (1,)),
                                    (k_ref, kr_scr, (0, 1)), (v_ref, vr_scr, (0, 1))):
        upcast(src_ref)
        for rho in range(d1):
            x = tmp_scr[pl.ds(rho, seg1, stride=d1), :]
            if 0 in slots:
                dst_scr[0, rho * seg1:(rho + 1) * seg1, :] = x.astype(BF16)
            if 1 in slots:
                tmp2_scr[rho * seg1:(rho + 1) * seg1, :] = x
        if 1 in slots:
            for rho in range(d1):
                for sub in range(d1):
                    r16 = rho + d1 * sub
                    dst_scr[1, r16 * seg2:(r16 + 1) * seg2, :] = (
                        tmp2_scr[pl.ds(rho * seg1 + sub, seg2, stride=d1), :].astype(BF16))

    def geometry(gi, rho, n):
        dil = DIL_GROUPS[gi][1]
        j = rho * (S // dil // blk) + n
        lo = (j - 1) * blk if n > 0 else j * blk
        nk = (j + 1) * blk - lo
        if gi == 0:
            srcs = (q0_ref, k_ref, v_ref)
        else:
            srcs = (qr_scr.at[gi - 1], kr_scr.at[gi - 1], vr_scr.at[gi - 1])
        return j, lo, nk, srcs

    def out_rows(gi, rho, n, start, size):
        dil = DIL_GROUPS[gi][1]
        first = rho + dil * (blk * n + start)
        if dil == 1:
            return gi, slice(first, first + size)
        if dil == d1:
            return gi, pl.ds(first, size, stride=dil)
        return n_groups, pl.ds((first % d1) * seg1 + first // d1, size, stride=dil // d1)

    def per_head(x, other):
        return jnp.concatenate([jnp.where(head0, x, other), jnp.where(head0, other, x)], axis=0)

    def scores(b, slot):
        j, lo, nk, (q_src, k_src, _) = geometry(*b)
        kc = k_src[lo:lo + nk, :]
        s_scr[slot, :, :2 * nk] = lax.dot_general(q_src[j * blk:(j + 1) * blk, :],
                                                  per_head(kc, jnp.zeros_like(kc)), _NT,
                                                  preferred_element_type=F32)

    def softmax(b, slot):
        gi = b[0]
        win, dil = DIL_GROUPS[gi]
        nk = geometry(*b)[2]
        rt = blk // 2
        qi = lax.broadcasted_iota(jnp.int32, (rt, nk), 0)
        kj = lax.broadcasted_iota(jnp.int32, (rt, nk), 1)
        for t in range(blk // rt):
            diff = (nk - blk) + (qi + t * rt) - kj
            valid = (diff >= 0) & (diff <= win // dil)
            ms = []
            for h in range(HEADS_PER_GROUP):
                s = jnp.where(valid, s_scr[slot, t * rt:(t + 1) * rt, h * nk:(h + 1) * nk], NEG_INF)
                m = jnp.max(s, axis=-1, keepdims=True)
                p_scr[slot, t * rt:(t + 1) * rt, h * nk:(h + 1) * nk] = jnp.exp2(s - m).astype(BF16)
                ms.append(m)
            slot_rows = out_rows(*b, t * rt, rt)
            m_scr[slot_rows[0], slot_rows[1], :] = jnp.where(head0, ms[0], ms[1])

    def weighted_values(b, slot):
        gi = b[0]
        _, lo, nk, (_, _, v_src) = geometry(*b)
        vc = v_src[lo:lo + nk, :]
        zero = jnp.zeros_like(vc)
        v2 = jnp.concatenate([per_head(vc, zero), head_ones(nk)], axis=1)
        pvl = _dot(p_scr[slot, :, :2 * nk], v2)
        slot, rows = out_rows(*b, 0, blk)
        acc_scr[slot, rows, :] = pvl[:, :LANES]
        l_scr[slot, rows, :] = pvl[:, LANES:]

    blocks = [(gi, rho, n) for gi, (_, dil) in enumerate(DIL_GROUPS)
              for rho in range(dil) for n in range(S // dil // blk)]
    gap = ATTN_PHASE_GAP
    n_slots = s_scr.shape[0]
    assert n_slots > 2 * gap
    for t in range(len(blocks) + 2 * gap):
        if 0 <= t - 2 * gap < len(blocks):
            weighted_values(blocks[t - 2 * gap], (t - 2 * gap) % n_slots)
        if t < len(blocks):
            scores(blocks[t], t % n_slots)
        if 0 <= t - gap < len(blocks):
            softmax(blocks[t - gap], (t - gap) % n_slots)

    for scr in (acc_scr, m_scr, l_scr):
        for rho in range(d1):
            scr[n_groups - 1, pl.ds(rho, seg1, stride=d1), :] = scr[n_groups, rho * seg1:(rho + 1) * seg1, :]

    def merge(t, carry):
        rows = pl.ds(pl.multiple_of(t * blk, blk), blk)
        ms = [m_scr[g, rows, :] for g in range(len(DIL_GROUPS))]
        m_all = functools.reduce(jnp.maximum, ms)
        ws = [jnp.exp2(m - m_all) for m in ms]
        num = functools.reduce(jnp.add, [w * acc_scr[g, rows, :] for g, w in enumerate(ws)])
        den = functools.reduce(jnp.add, [w * l_scr[g, rows, :] for g, w in enumerate(ws)])
        o_ref[rows, :] = (num / den * sg_ref[rows, :].astype(F32)).astype(o_ref.dtype)
        return carry

    lax.fori_loop(0, S // blk, merge, 0)


def _attn(q, k_sh, v_sh, sg):
    B, S, D = k_sh.shape
    n_groups = len(DIL_GROUPS)
    n_pairs = D // LANES
    qspec = lambda g: pl.BlockSpec((None, S, LANES), lambda b, p: (b, 0, g * n_pairs + p))
    blk = pl.BlockSpec((None, S, LANES), lambda b, p: (b, 0, p))
    res = lambda: pltpu.VMEM((n_groups - 1, S, LANES), BF16)
    nat = lambda: pltpu.VMEM((n_groups + 1, S, LANES), F32)
    return pl.pallas_call(
        _attn_kernel,
        grid=(B, n_pairs),
        in_specs=[qspec(0), qspec(1), qspec(2), blk, blk, blk],
        out_specs=blk,
        out_shape=jax.ShapeDtypeStruct((B, S, D), BF16),
        scratch_shapes=[pltpu.VMEM((S, LANES), F32), pltpu.VMEM((S, LANES), F32), res(), res(), res(),
                        nat(), nat(), nat(),
                        pltpu.VMEM((ATTN_SLOTS, BAND_BLOCK, 2 * HEADS_PER_GROUP * BAND_BLOCK), F32),
                        pltpu.VMEM((ATTN_SLOTS, BAND_BLOCK, 2 * HEADS_PER_GROUP * BAND_BLOCK), BF16)],
        compiler_params=pltpu.CompilerParams(vmem_limit_bytes=VMEM_LIMIT),
        name="attn",
    )(q, q, q, k_sh, v_sh, sg)


def _b_post_kernel(att_ref, x_ref, mod_ref, wout_ref, out_ref):
    D = x_ref.shape[-1]
    gate = mod_ref[...][:, 2 * D:]
    out_ref[...] = x_ref[...] + gate * _dot(att_ref[...], wout_ref[...])


def _b_post(att, xr, mod3, w_out):
    B, S, D = xr.shape
    ts = SEQ_TILE
    tile = pl.BlockSpec((None, ts, D), lambda b, i: (b, i, 0))
    return pl.pallas_call(
        _b_post_kernel,
        grid=(B, S // ts),
        in_specs=[tile, tile, pl.BlockSpec((None, 1, 3 * D), lambda b, i: (b, 0, 0)),
                  pl.BlockSpec(w_out.shape, lambda b, i: (0, 0))],
        out_specs=tile,
        out_shape=jax.ShapeDtypeStruct((B, S, D), F32),
        compiler_params=pltpu.CompilerParams(vmem_limit_bytes=VMEM_LIMIT),
        name="b_post",
    )(att, xr, mod3, w_out)


def _rope_tables(seq):
    pos = jnp.arange(seq, dtype=F32)
    inv = ROPE_THETA ** (-jnp.arange(0, HEAD_DIM, 2, dtype=F32) / HEAD_DIM)
    ang = pos[:, None] * inv[None, :]
    cos, sin = jnp.cos(ang), jnp.sin(ang)
    cos_t = jnp.concatenate([cos, cos] * HEADS_PER_GROUP, axis=-1)
    sin_t = jnp.concatenate([-sin, sin] * HEADS_PER_GROUP, axis=-1)
    return cos_t, sin_t


def kernel(x, c, a_ada_w, a_ada_b, a_norm_g, a_mix_mu, a_w_in, a_w0, a_w1, a_w2, a_a0, a_a1, a_a2,
           a_k_k, a_k_a, a_r_k, a_ln_g, a_ln_b, a_w_out, kv_norm_g, w_kv, k_norm_g,
           b_ada_w, b_ada_b, b_norm_g, b_w_in, b_q_norm_g, b_w_out):
    B, S, D = x.shape
    assert a_ada_w.shape[0] == 1 and b_ada_w.shape[0] == 1
    assert D % LANES == 0 and S % (BAND_BLOCK * DIL_GROUPS[-1][1]) == 0 and S % SEQ_TILE == 0
    assert all(win // dil == BAND_BLOCK for win, dil in DIL_GROUPS)
    assert b_w_in.shape[-1] == (len(DIL_GROUPS) + 1) * D
    row = lambda t: t.reshape(1, -1)
    per_head = lambda t: jnp.tile(t.reshape(1, HEAD_DIM), (1, D // HEAD_DIM))

    mod_a, mod_b = _adaln(c, a_ada_w, a_ada_b, b_ada_w, b_ada_b)
    mod_a = mod_a.reshape(B, 1, 3 * D)
    mod_b = mod_b.reshape(B, 1, 3 * D)
    cos_t, sin_t = _rope_tables(S)

    r, lw, k, v, an, bn, sg_a = _a_pre(
        x, mod_a, a_norm_g, a_mix_mu[0], a_w_in[0].astype(BF16), a_w0, a_w1[0].astype(BF16),
        a_w2[0].astype(BF16), a_a0, a_a1[0].astype(BF16), a_a2[0].astype(BF16), a_k_k, a_k_a)
    y = _wkv(r, lw, k, v, an, bn)
    xr, k_sh, v_sh = _a_post(
        y, r, k, v, sg_a, x, mod_a, a_ln_g, a_ln_b, row(a_r_k[0]), a_w_out[0].astype(BF16),
        row(kv_norm_g), w_kv.astype(BF16), per_head(k_norm_g), cos_t, sin_t)

    q, sg_b = _b_pre(xr, mod_b, b_norm_g, b_w_in[0].astype(BF16), per_head(b_q_norm_g[0]), cos_t, sin_t)
    att = _attn(q, k_sh, v_sh, sg_b)
    return _b_post(att, xr, mod_b, b_w_out[0].astype(BF16))
```

```python
import functools

import jax
import jax.numpy as jnp
from jax import lax
from jax.experimental import pallas as pl
from jax.experimental.pallas import tpu as pltpu

F32 = jnp.float32
BF16 = jnp.bfloat16

HEAD_DIM = 64
LANES = 128
HEADS_PER_GROUP = LANES // HEAD_DIM
DIL_GROUPS = ((128, 1), (512, 4), (2048, 16))
BAND_BLOCK = 128
ROPE_THETA = 10000.0
NORM_EPS = 1e-6
GN_EPS = 64e-5
NEG_INF = -1e30
DECAY_SCALE = 0.6065306597126334
CHUNK = 64
WKV_PHASE_GAP = 4
WKV_SCAN_GAP = 2
ATTN_PHASE_GAP = 2
ATTN_SLOTS = 8
SEQ_TILE = 512
Q_SCALE = HEAD_DIM ** -0.5 * 1.4426950408889634
VMEM_LIMIT = 56 * 1024 * 1024

_NT = (((1,), (1,)), ((), ()))
_TN = (((0,), (0,)), ((), ()))


def _dot(a, b):
    return jnp.dot(a, b, preferred_element_type=F32)


def _mm(a, b):
    return _dot(a.astype(BF16), b.astype(BF16))


def _split_dot_lhs(e, x):
    hi = x.astype(BF16)
    lo = (x - hi.astype(F32)).astype(BF16)
    return _dot(e, hi) + _dot(e, lo)


def _mm_3pass(a, b):
    a_hi = a.astype(BF16)
    a_lo = (a - a_hi.astype(F32)).astype(BF16)
    b_hi = b.astype(BF16)
    b_lo = (b - b_hi.astype(F32)).astype(BF16)
    return _dot(a_hi, b_hi) + (_dot(a_hi, b_lo) + _dot(a_lo, b_hi))


def _head_sum(x, e):
    parts = [_dot(x[:, g * LANES:(g + 1) * LANES].astype(BF16), e) for g in range(x.shape[1] // LANES)]
    return parts[0] if len(parts) == 1 else jnp.concatenate(parts, axis=1)


def _rot_half(x):
    lane = lax.broadcasted_iota(jnp.int32, (1, LANES), 1)
    first = (lane % HEAD_DIM) < (HEAD_DIM // 2)
    parts = []
    for g in range(x.shape[1] // LANES):
        xg = x[:, g * LANES:(g + 1) * LANES]
        up = pltpu.roll(xg, LANES - HEAD_DIM // 2, axis=1)
        dn = pltpu.roll(xg, HEAD_DIM // 2, axis=1)
        parts.append(jnp.where(first, up, dn))
    return parts[0] if len(parts) == 1 else jnp.concatenate(parts, axis=1)


def _tile_lanes(t, n):
    return t if n == 1 else jnp.concatenate([t] * n, axis=1)


def _rms_rows(x):
    return x * lax.rsqrt(jnp.mean(x * x, axis=-1, keepdims=True) + NORM_EPS)


def _sigmoid(x):
    return 1.0 / (1.0 + jnp.exp(-x))


def _head_ones():
    r = lax.broadcasted_iota(jnp.int32, (LANES, LANES), 0) // HEAD_DIM
    c = lax.broadcasted_iota(jnp.int32, (LANES, LANES), 1) // HEAD_DIM
    return (r == c).astype(BF16)


def _adaln_kernel(c_ref, wa_ref, ba_ref, wb_ref, bb_ref, oa_ref, ob_ref):
    c = c_ref[...]
    sc = c * _sigmoid(c)
    oa_ref[...] = _mm_3pass(sc, wa_ref[...]) + ba_ref[...]
    ob_ref[...] = _mm_3pass(sc, wb_ref[...]) + bb_ref[...]


def _adaln(c, wa, ba, wb, bb):
    B, D = c.shape
    n3 = wa.shape[-1]
    tn = 512
    wspec = pl.BlockSpec((None, D, tn), lambda j: (0, 0, j))
    bspec = pl.BlockSpec((1, tn), lambda j: (0, j))
    ospec = pl.BlockSpec((B, tn), lambda j: (0, j))
    return pl.pallas_call(
        _adaln_kernel,
        grid=(n3 // tn,),
        in_specs=[pl.BlockSpec((B, D), lambda j: (0, 0)), wspec, bspec, wspec, bspec],
        out_specs=[ospec, ospec],
        out_shape=[jax.ShapeDtypeStruct((B, n3), F32)] * 2,
        name="adaln",
    )(c, wa, ba, wb, bb)


def _a_pre_kernel(x_ref, xp_ref, mod_ref, ng_ref, mu_ref, win_ref, w0_ref, w1_ref, w2_ref,
                  a0_ref, a1_ref, a2_ref, kk_ref, ka_ref,
                  r_out, lw_out, k_out, v_out, an_out, bn_out, sg_out):
    D = x_ref.shape[-1]
    i = pl.program_id(1)
    mod = mod_ref[...]
    shift, scale = mod[:, :D], mod[:, D:2 * D]
    gain = ng_ref[...] * (1.0 + scale)

    h = _rms_rows(x_ref[...]) * gain + shift
    hp = _rms_rows(xp_ref[...]) * gain + shift
    hp = jnp.where(i == 0, 0.0, hp[7:8, :])
    row = lax.broadcasted_iota(jnp.int32, h.shape, 0)
    hs = jnp.where(row == 0, hp, pltpu.roll(h, 1, axis=0))
    xx = hs - h
    mu = mu_ref[...]
    lerp = lambda p: (h + xx * mu[p:p + 1, :]).astype(BF16)
    wl = _mm(jnp.tanh(_mm(lerp(4), w1_ref[...])), w2_ref[...])
    al = _mm(_mm(lerp(5), a1_ref[...]), a2_ref[...])
    r = _dot(lerp(0), win_ref[:, 0 * D:1 * D])
    k = _dot(lerp(1), win_ref[:, 1 * D:2 * D])
    v = _dot(lerp(2), win_ref[:, 2 * D:3 * D])
    g = _dot(lerp(3), win_ref[:, 3 * D:4 * D])

    a = _sigmoid(a0_ref[...] + al)

    kk = k * kk_ref[...]
    ss = _head_sum(kk * kk, _head_ones())
    kk = kk * lax.rsqrt(jnp.maximum(ss, 1e-24))

    r_out[...] = r.astype(r_out.dtype)
    lw_out[...] = (-DECAY_SCALE) * _sigmoid(w0_ref[...] + wl)
    k_out[...] = (k * (1.0 + (a - 1.0) * ka_ref[...])).astype(k_out.dtype)
    v_out[...] = v.astype(v_out.dtype)
    an_out[...] = (-kk).astype(an_out.dtype)
    bn_out[...] = (kk * a).astype(bn_out.dtype)
    sg_out[...] = (g * _sigmoid(g)).astype(sg_out.dtype)


def _a_pre(x, mod3, ng, mu, w_in, w0, w1, w2, a0, a1, a2, k_k, k_a):
    B, S, D = x.shape
    ts = SEQ_TILE
    tile = pl.BlockSpec((None, ts, D), lambda b, i: (b, i, 0))
    prev = pl.BlockSpec((None, 8, D), lambda b, i: (b, jnp.maximum(i * (ts // 8) - 1, 0), 0))
    vec = pl.BlockSpec((1, D), lambda b, i: (0, 0))

    def full(a):
        return pl.BlockSpec(a.shape, lambda b, i: (0,) * a.ndim, pipeline_mode=pl.Buffered(1))

    act = lambda dt: jax.ShapeDtypeStruct((B, S, D), dt)
    return pl.pallas_call(
        _a_pre_kernel,
        grid=(B, S // ts),
        in_specs=[tile, prev, pl.BlockSpec((None, 1, 3 * D), lambda b, i: (b, 0, 0)), vec, full(mu),
                  full(w_in), vec, full(w1), full(w2), vec, full(a1), full(a2), vec, vec],
        out_specs=[tile] * 7,
        out_shape=[act(BF16), act(F32), act(BF16), act(BF16), act(BF16), act(BF16), act(BF16)],
        compiler_params=pltpu.CompilerParams(vmem_limit_bytes=VMEM_LIMIT),
        name="a_pre",
    )(x, x, mod3, ng, mu, w_in, w0, w1, w2, a0, a1, a2, k_k, k_a)


def _wkv_kernel(r_ref, lw_ref, k_ref, v_ref, a_ref, b_ref, y_ref,
                lp_scr, t_scr, aak_scr, ark_scr, arb_scr, ats_scr, vs_scr, rt_scr, kend_scr, bend_scr,
                wend_scr, rh_scr, y0_scr, p_scr, q_scr, st_scr):
    C = CHUNK
    n_chunks = r_ref.shape[0] // C
    assert C == HEAD_DIM
    step = pl.program_id(0)
    cur = step % 2
    prev = 1 - cur

    lane = lax.broadcasted_iota(jnp.int32, (1, LANES), 1)
    head0 = lane < HEAD_DIM
    ti = lax.broadcasted_iota(jnp.int32, (C, LANES), 0)
    ii = lax.broadcasted_iota(jnp.int32, (C, LANES), 1) % C
    strict = ii < ti
    incl = ii <= ti
    eye_pair = (ii == ti).astype(F32)
    ri = lax.broadcasted_iota(jnp.int32, (LANES, LANES), 0)
    ci = lax.broadcasted_iota(jnp.int32, (LANES, LANES), 1)
    same = (ri // HEAD_DIM) == (ci // HEAD_DIM)
    tri = (lax.broadcasted_iota(jnp.int32, (C, C), 1)
           <= lax.broadcasted_iota(jnp.int32, (C, C), 0)).astype(BF16)

    def stack(x):
        zero = jnp.zeros_like(x)
        return jnp.concatenate([jnp.where(head0, x, zero), jnp.where(head0, zero, x)], axis=0)

    def chunk_rows(c):
        return slice(c * C, (c + 1) * C)

    def gram_terms(c):
        rows = chunk_rows(c)
        cum = _split_dot_lhs(tri, lw_ref[rows, :])
        yield
        lw = lw_ref[rows, :]
        r, k, v, a, b = (ref[rows, :].astype(F32) for ref in (r_ref, k_ref, v_ref, a_ref, b_ref))
        e_cum = jnp.exp(cum)
        e_inv = jnp.exp(-cum)
        e_end = jnp.exp(cum[C - 1:C, :] - cum)
        rt = r * e_cum
        at = (a * jnp.exp(cum - lw)).astype(BF16)
        kt = (k * e_inv).astype(BF16)
        bt = (b * e_inv).astype(BF16)
        kb = jnp.concatenate([stack(kt), stack(bt)], axis=0)
        ga = lax.dot_general(at, kb, _NT, preferred_element_type=F32)
        gr = lax.dot_general(rt.astype(BF16), kb, _NT, preferred_element_type=F32)
        ats_scr[c] = stack(at)
        vs_scr[c] = stack(v.astype(BF16))
        rt_scr[rows, :] = rt
        kend_scr[rows, :] = (k * e_end).astype(BF16)
        bend_scr[rows, :] = (b * e_end).astype(BF16)
        wend_scr[c] = jnp.broadcast_to(e_cum[C - 1:C, :], (8, LANES))
        yield
        a_ab = jnp.where(strict, ga[:, LANES:], 0.0)
        lp_scr[c] = a_ab.astype(BF16)
        t_scr[c] = eye_pair + a_ab
        aak_scr[c] = jnp.where(strict, ga[:, :LANES], 0.0).astype(BF16)
        ark_scr[c] = jnp.where(incl, gr[:, :LANES], 0.0).astype(BF16)
        arb_scr[c] = jnp.where(incl, gr[:, LANES:], 0.0).astype(BF16)

    def square_only(c):
        lp = lp_scr[c]
        lp2 = _dot(lp, stack(lp))
        yield
        lp_scr[c] = lp2.astype(BF16)

    def fold_and_square(c):
        lp = lp_scr[c]
        lp_bd = stack(lp)
        t = t_scr[c]
        tl = _dot(t.astype(BF16), lp_bd)
        lp2 = _dot(lp, lp_bd)
        yield
        t_scr[c] = t + tl
        lp_scr[c] = lp2.astype(BF16)

    def fold_only(c):
        t = t_scr[c]
        tl = _dot(t.astype(BF16), stack(lp_scr[c]))
        yield
        t_scr[c] = t + tl

    def chunk_maps(c):
        rows = chunk_rows(c)
        vs = vs_scr[c]
        x = _dot(aak_scr[c], vs)
        y0a = _dot(ark_scr[c], vs)
        yield
        au = _dot(t_scr[c].astype(BF16),
                  jnp.concatenate([ats_scr[c], stack(x.astype(BF16))], axis=1))
        yield
        ah = au[:, :LANES].astype(BF16)
        u0 = au[:, LANES:].astype(BF16)
        ry = _dot(arb_scr[c], jnp.concatenate([stack(ah), stack(u0)], axis=1))
        k_end = kend_scr[rows, :]
        b_end = bend_scr[rows, :]
        p = lax.dot_general(b_end, ah, _TN, preferred_element_type=F32)
        q = lax.dot_general(jnp.concatenate([k_end, b_end], axis=0),
                            jnp.concatenate([v_ref[rows, :], u0], axis=0), _TN,
                            preferred_element_type=F32)
        yield
        rh_scr[cur, rows, :] = rt_scr[rows, :] + ry[:, :LANES]
        y0_scr[cur, rows, :] = y0a + ry[:, LANES:]
        p = jnp.where(same, p, 0.0)
        p_pair = p[:C, :] + p[C:, :] + eye_pair * wend_scr[c][0:1, :]
        p_hi = p_pair.astype(BF16)
        p_scr[cur, c] = jnp.concatenate([p_hi, (p_pair - p_hi.astype(F32)).astype(BF16)], axis=1)
        q = jnp.where(same, q, 0.0)
        q_scr[cur, c] = q[:C, :] + q[C:, :]

    def scan_chunks():
        for c in range(n_chunks):
            rows = chunk_rows(c)
            st = st_scr[...]
            st_hi = st.astype(BF16)
            st_lo = (st - st_hi.astype(F32)).astype(BF16)
            hi_bd = stack(st_hi)
            pc = p_scr[prev, c]
            p_hi = pc[:, :LANES]
            st_new = _dot(p_hi, hi_bd) + (_dot(p_hi, stack(st_lo)) + _dot(pc[:, LANES:], hi_bd))
            ys = _dot(rh_scr[prev, rows, :].astype(BF16), hi_bd)
            yield
            y_ref[rows, :] = (y0_scr[prev, rows, :] + ys).astype(y_ref.dtype)
            st_scr[...] = st_new + q_scr[prev, c]

    stages = [gram_terms, square_only]
    n = 4
    while n < C:
        stages.append(fold_and_square)
        n *= 2
    stages += [fold_only, chunk_maps]

    @pl.when(step == 0)
    def _():
        rh_scr[1] = jnp.zeros(rh_scr.shape[1:], F32)
        y0_scr[1] = jnp.zeros(y0_scr.shape[1:], F32)
        p_scr[1] = jnp.zeros(p_scr.shape[1:], BF16)
        q_scr[1] = jnp.zeros(q_scr.shape[1:], F32)

    def chunk_pipeline(c):
        for stage in stages:
            yield from stage(c)

    st_scr[...] = jnp.zeros((C, LANES), F32)
    gens =[chunk_pipeline(c) for c in range(n_chunks)]
    scan = scan_chunks()
    live = set(range(n_chunks))
    t = 0
    scan_live = True
    while live or scan_live:
        for i in sorted(live):
            if t >= i and (t - i) % WKV_PHASE_GAP == 0:
                try:
                    next(gens[i])
                except StopIteration:
                    live.discard(i)
        if scan_live and t % WKV_SCAN_GAP == 0:
            try:
                next(scan)
            except StopIteration:
                scan_live = False
        t += 1


def _wkv(r, lw, k, v, an, bn):
    B, S, D = r.shape
    n_pairs = D // LANES
    n_steps = B * n_pairs
    n_chunks = S // CHUNK
    in_blk = pl.BlockSpec((None, S, LANES), lambda s: (jnp.minimum(s, n_steps - 1) // n_pairs, 0,
                                                       jnp.minimum(s, n_steps - 1) % n_pairs))
    out_blk = pl.BlockSpec((None, S, LANES), lambda s: (jnp.maximum(s - 1, 0) // n_pairs, 0,
                                                        jnp.maximum(s - 1, 0) % n_pairs))
    mat = lambda dt: pltpu.VMEM((n_chunks, LANES, LANES), dt)
    pair = lambda dt: pltpu.VMEM((n_chunks, CHUNK, LANES), dt)
    seq = lambda dt: pltpu.VMEM((S, LANES), dt)
    return pl.pallas_call(
        _wkv_kernel,
        grid=(n_steps + 1,),
        in_specs=[in_blk] * 6,
        out_specs=out_blk,
        out_shape=jax.ShapeDtypeStruct((B, S, D), BF16),
        scratch_shapes=[pair(BF16), pair(F32), pair(BF16), pair(BF16), pair(BF16), mat(BF16), mat(BF16),
                        seq(F32), seq(BF16), seq(BF16), pltpu.VMEM((n_chunks, 8, LANES), F32),
                        pltpu.VMEM((2, S, LANES), F32), pltpu.VMEM((2, S, LANES), F32),
                        pltpu.VMEM((2, n_chunks, CHUNK, 2 * LANES), BF16),
                        pltpu.VMEM((2, n_chunks, CHUNK, LANES), F32),
                        pltpu.VMEM((CHUNK, LANES), F32)],
        compiler_params=pltpu.CompilerParams(vmem_limit_bytes=VMEM_LIMIT,
                                             dimension_semantics=("arbitrary",)),
        name="wkv",
    )(r, lw, k, v, an, bn)


def _a_post_kernel(y_ref, r_ref, k_ref, v_ref, sg_ref, x_ref, mod_ref, lng_ref, lnb_ref, rk_ref,
                   wout_ref, kvg_ref, wkv_ref, kng_ref, cos_ref, sin_ref,
                   xr_out, ksh_out, vsh_out):
    D = x_ref.shape[-1]
    e = _head_ones()
    n_lane_groups = D // LANES
    f32 = lambda ref: ref[...].astype(F32)
    y = f32(y_ref)
    mean = _head_sum(y, e) * (1.0 / HEAD_DIM)
    d = y - mean
    var = _head_sum(d * d, e) * (1.0 / HEAD_DIM)
    yn = d * lax.rsqrt(var + GN_EPS) * lng_ref[...] + lnb_ref[...]
    bonus = _head_sum(f32(r_ref) * f32(k_ref) * rk_ref[...], e) * f32(v_ref)
    mix = _mm((yn + bonus) * f32(sg_ref), wout_ref[...])
    gate = mod_ref[...][:, 2 * D:]
    xr = x_ref[...] + gate * mix
    xr_out[...] = xr

    kv = _mm(_rms_rows(xr) * kvg_ref[...], wkv_ref[...])
    ks = kv[:, :D]
    ks = ks * lax.rsqrt(_head_sum(ks * ks, e) * (1.0 / HEAD_DIM) + NORM_EPS) * kng_ref[...]
    cos = _tile_lanes(cos_ref[...], n_lane_groups)
    sin = _tile_lanes(sin_ref[...], n_lane_groups)
    ksh_out[...] = (ks * cos + _rot_half(ks) * sin).astype(ksh_out.dtype)
    vsh_out[...] = kv[:, D:].astype(vsh_out.dtype)


def _a_post(y, r, k, v, sg, x, mod3, ln_g, ln_b, r_k, w_out, kv_g, w_kv, kn_g, cos_t, sin_t):
    B, S, D = x.shape
    ts = SEQ_TILE
    tile = pl.BlockSpec((None, ts, D), lambda b, i: (b, i, 0))
    vec = pl.BlockSpec((1, D), lambda b, i: (0, 0))
    rope = pl.BlockSpec((ts, LANES), lambda b, i: (i, 0))

    def full(a):
        return pl.BlockSpec(a.shape, lambda b, i: (0,) * a.ndim)

    act = lambda dt: jax.ShapeDtypeStruct((B, S, D), dt)
    return pl.pallas_call(
        _a_post_kernel,
        grid=(B, S // ts),
        in_specs=[tile] * 6 + [pl.BlockSpec((None, 1, 3 * D), lambda b, i: (b, 0, 0)), vec, vec, vec,
                               full(w_out), vec, full(w_kv), vec, rope, rope],
        out_specs=[tile] * 3,
        out_shape=[act(F32), act(BF16), act(BF16)],
        compiler_params=pltpu.CompilerParams(vmem_limit_bytes=VMEM_LIMIT),
        name="a_post",
    )(y, r, k, v, sg, x, mod3, ln_g, ln_b, r_k, w_out, kv_g, w_kv, kn_g, cos_t, sin_t)


def _b_pre_kernel(x_ref, mod_ref, ng_ref, win_ref, qg_ref, cos_ref, sin_ref, q_out, sg_out):
    D = x_ref.shape[-1]
    nq = q_out.shape[-1]
    e = _head_ones()
    mod = mod_ref[...]
    shift, scale = mod[:, :D], mod[:, D:2 * D]
    h = (_rms_rows(x_ref[...]) * (ng_ref[...] * (1.0 + scale)) + shift).astype(BF16)
    cos = _tile_lanes(cos_ref[...], D // LANES)
    sin = _tile_lanes(sin_ref[...], D // LANES)
    for g in range(nq // D):
        q = _dot(h, win_ref[:, g * D:(g + 1) * D])
        q = q * lax.rsqrt(_head_sum(q * q, e) * (1.0 / HEAD_DIM) + NORM_EPS) * qg_ref[...]
        q_out[:, g * D:(g + 1) * D] = ((q * cos + _rot_half(q) * sin) * Q_SCALE).astype(q_out.dtype)
    gate = _dot(h, win_ref[:, nq:])
    sg_out[...] = (gate * _sigmoid(gate)).astype(sg_out.dtype)


def _b_pre(xr, mod3, ng, w_in, qn_g, cos_t, sin_t):
    B, S, D = xr.shape
    nq = w_in.shape[1] - D
    ts = SEQ_TILE
    tile = pl.BlockSpec((None, ts, D), lambda b, i: (b, i, 0))
    vec = pl.BlockSpec((1, D), lambda b, i: (0, 0))
    rope = pl.BlockSpec((ts, LANES), lambda b, i: (i, 0))
    return pl.pallas_call(
        _b_pre_kernel,
        grid=(B, S // ts),
        in_specs=[tile, pl.BlockSpec((None, 1, 3 * D), lambda b, i: (b, 0, 0)), vec,
                  pl.BlockSpec(w_in.shape, lambda b, i: (0, 0)), vec, rope, rope],
        out_specs=[pl.BlockSpec((None, ts, nq), lambda b, i: (b, i, 0)), tile],
        out_shape=[jax.ShapeDtypeStruct((B, S, nq), BF16), jax.ShapeDtypeStruct((B, S, D), BF16)],
        compiler_params=pltpu.CompilerParams(vmem_limit_bytes=VMEM_LIMIT),
        name="b_pre",
    )(xr, mod3, ng, w_in, qn_g, cos_t, sin_t)


def _attn_kernel(q0_ref, q1_ref, q2_ref, k_ref, v_ref, sg_ref, o_ref,
                 tmp_scr, tmp2_scr, qr_scr, kr_scr, vr_scr, acc_scr, m_scr, l_scr, s_scr, p_scr):
    S = k_ref.shape[0]
    blk = BAND_BLOCK
    n_groups = len(DIL_GROUPS)
    lane = lax.broadcasted_iota(jnp.int32, (1, LANES), 1)
    head0 = lane < HEAD_DIM

    def head_ones(nk):
        hsel = (lax.broadcasted_iota(jnp.int32, (HEADS_PER_GROUP * nk, LANES), 1) // HEAD_DIM
                == lax.broadcasted_iota(jnp.int32, (HEADS_PER_GROUP * nk, LANES), 0) // nk)
        return hsel.astype(F32).astype(BF16)

    def upcast(src_ref):
        rows_per = 8 * blk

        def tile(t, carry):
            rows = pl.ds(pl.multiple_of(t * rows_per, rows_per), rows_per)
            tmp_scr[rows, :] = src_ref[rows, :].astype(F32)
            return carry
        lax.fori_loop(0, S // rows_per, tile, 0)

    (_, d1), (_, d2) = DIL_GROUPS[1], DIL_GROUPS[2]
    assert d2 == d1 * d1 and DIL_GROUPS[0][1] == 1
    seg1, seg2 = S // d1, S // d2
    for src_ref, dst_scr, slots in ((q1_ref, qr_scr, (0,)), (q2_ref, qr_scr, (1,)),
                                    (k_ref, kr_scr, (0, 1)), (v_ref, vr_scr, (0, 1))):
        upcast(src_ref)
        for rho in range(d1):
            x = tmp_scr[pl.ds(rho, seg1, stride=d1), :]
            if 0 in slots:
                dst_scr[0, rho * seg1:(rho + 1) * seg1, :] = x.astype(BF16)
            if 1 in slots:
                tmp2_scr[rho * seg1:(rho + 1) * seg1, :] = x
        if 1 in slots:
            for rho in range(d1):
                for sub in range(d1):
                    r16 = rho + d1 * sub
                    dst_scr[1, r16 * seg2:(r16 + 1) * seg2, :] = (
                        tmp2_scr[pl.ds(rho * seg1 + sub, seg2, stride=d1), :].astype(BF16))

    def geometry(gi, rho, n):
        dil = DIL_GROUPS[gi][1]
        j = rho * (S // dil // blk) + n
        lo = (j - 1) * blk if n > 0 else j * blk
        nk = (j + 1) * blk - lo
        if gi == 0:
            srcs = (q0_ref, k_ref, v_ref)
        else:
            srcs = (qr_scr.at[gi - 1], kr_scr.at[gi - 1], vr_scr.at[gi - 1])
        return j, lo, nk, srcs

    def out_rows(gi, rho, n, start, size):
        dil = DIL_GROUPS[gi][1]
        first = rho + dil * (blk * n + start)
        if dil == 1:
            return gi, slice(first, first + size)
        if dil == d1:
            return gi, pl.ds(first, size, stride=dil)
        return n_groups, pl.ds((first % d1) * seg1 + first // d1, size, stride=dil // d1)

    def per_head(x, other):
        return jnp.concatenate([jnp.where(head0, x, other), jnp.where(head0, other, x)], axis=0)

    def scores(b, slot):
        j, lo, nk, (q_src, k_src, _) = geometry(*b)
        kc = k_src[lo:lo + nk, :]
        s_scr[slot, :, :2 * nk] = lax.dot_general(q_src[j * blk:(j + 1) * blk, :],
                                                  per_head(kc, jnp.zeros_like(kc)), _NT,
                                                  preferred_element_type=F32)

    def softmax(b, slot):
        gi = b[0]
        win, dil = DIL_GROUPS[gi]
        nk = geometry(*b)[2]
        rt = blk // 2
        qi = lax.broadcasted_iota(jnp.int32, (rt, nk), 0)
        kj = lax.broadcasted_iota(jnp.int32, (rt, nk), 1)
        for t in range(blk // rt):
            diff = (nk - blk) + (qi + t * rt) - kj
            valid = (diff >= 0) & (diff <= win // dil)
            ms = []
            for h in range(HEADS_PER_GROUP):
                s = jnp.where(valid, s_scr[slot, t * rt:(t + 1) * rt, h * nk:(h + 1) * nk], NEG_INF)
                m = jnp.max(s, axis=-1, keepdims=True)
                p_scr[slot, t * rt:(t + 1) * rt, h * nk:(h + 1) * nk] = jnp.exp2(s - m).astype(BF16)
                ms.append(m)
            slot_rows = out_rows(*b, t * rt, rt)
            m_scr[slot_rows[0], slot_rows[1], :] = jnp.where(head0, ms[0], ms[1])

    def weighted_values(b, slot):
        gi = b[0]
        _, lo, nk, (_, _, v_src) = geometry(*b)
        vc = v_src[lo:lo + nk, :]
        zero = jnp.zeros_like(vc)
        v2 = jnp.concatenate([per_head(vc, zero), head_ones(nk)], axis=1)
        pvl = _dot(p_scr[slot, :, :2 * nk], v2)
        slot, rows = out_rows(*b, 0, blk)
        acc_scr[slot, rows, :] = pvl[:, :LANES]
        l_scr[slot, rows, :] = pvl[:, LANES:]

    blocks = [(gi, rho, n) for gi, (_, dil) in enumerate(DIL_GROUPS)
              for rho in range(dil) for n in range(S // dil // blk)]
    gap = ATTN_PHASE_GAP
    n_slots = s_scr.shape[0]
    assert n_slots > 2 * gap
    for t in range(len(blocks) + 2 * gap):
        if 0 <= t - 2 * gap < len(blocks):
            weighted_values(blocks[t - 2 * gap], (t - 2 * gap) % n_slots)
        if t < len(blocks):
            scores(blocks[t], t % n_slots)
        if 0 <= t - gap < len(blocks):
            softmax(blocks[t - gap], (t - gap) % n_slots)

    for scr in (acc_scr, m_scr, l_scr):
        for rho in range(d1):
            scr[n_groups - 1, pl.ds(rho, seg1, stride=d1), :] = scr[n_groups, rho * seg1:(rho + 1) * seg1, :]

    merge_rows = 4 * blk

    def merge(t, carry):
        rows = pl.ds(pl.multiple_of(t * merge_rows, merge_rows), merge_rows)
        ms = [m_scr[g, rows, :] for g in range(n_groups)]
        m_all = functools.reduce(jnp.maximum, ms)
        ws = [jnp.exp2(m - m_all) for m in ms]
        num = functools.reduce(jnp.add, [w * acc_scr[g, rows, :] for g, w in enumerate(ws)])
        den = functools.reduce(jnp.add, [w * l_scr[g, rows, :] for g, w in enumerate(ws)])
        o_ref[rows, :] = (num / den * sg_ref[rows, :].astype(F32)).astype(o_ref.dtype)
        return carry

    lax.fori_loop(0, S // merge_rows, merge, 0)


def _attn(q, k_sh, v_sh, sg):
    B, S, D = k_sh.shape
    n_groups = len(DIL_GROUPS)
    n_pairs = D // LANES
    qspec = lambda g: pl.BlockSpec((None, S, LANES), lambda b, p: (b, 0, g * n_pairs + p))
    blk = pl.BlockSpec((None, S, LANES), lambda b, p: (b, 0, p))
    res = lambda: pltpu.VMEM((n_groups - 1, S, LANES), BF16)
    nat = lambda: pltpu.VMEM((n_groups + 1, S, LANES), F32)
    return pl.pallas_call(
        _attn_kernel,
        grid=(B, n_pairs),
        in_specs=[qspec(0), qspec(1), qspec(2), blk, blk, blk],
        out_specs=blk,
        out_shape=jax.ShapeDtypeStruct((B, S, D), BF16),
        scratch_shapes=[pltpu.VMEM((S, LANES), F32), pltpu.VMEM((S, LANES), F32), res(), res(), res(),
                        nat(), nat(), nat(),
                        pltpu.VMEM((ATTN_SLOTS, BAND_BLOCK, 2 * HEADS_PER_GROUP * BAND_BLOCK), F32),
                        pltpu.VMEM((ATTN_SLOTS, BAND_BLOCK, 2 * HEADS_PER_GROUP * BAND_BLOCK), BF16)],
        compiler_params=pltpu.CompilerParams(vmem_limit_bytes=VMEM_LIMIT),
        name="attn",
    )(q, q, q, k_sh, v_sh, sg)


def _b_post_kernel(att_ref, x_ref, mod_ref, wout_ref, out_ref):
    D = x_ref.shape[-1]
    gate = mod_ref[...][:, 2 * D:]
    out_ref[...] = x_ref[...] + gate * _dot(att_ref[...], wout_ref[...])


def _b_post(att, xr, mod3, w_out):
    B, S, D = xr.shape
    ts = SEQ_TILE
    tile = pl.BlockSpec((None, ts, D), lambda b, i: (b, i, 0))
    return pl.pallas_call(
        _b_post_kernel,
        grid=(B, S // ts),
        in_specs=[tile, tile, pl.BlockSpec((None, 1, 3 * D), lambda b, i: (b, 0, 0)),
                  pl.BlockSpec(w_out.shape, lambda b, i: (0, 0))],
        out_specs=tile,
        out_shape=jax.ShapeDtypeStruct((B, S, D), F32),
        compiler_params=pltpu.CompilerParams(vmem_limit_bytes=VMEM_LIMIT),
        name="b_post",
    )(att, xr, mod3, w_out)


def _rope_tables(seq):
    pos = jnp.arange(seq, dtype=F32)
    inv = ROPE_THETA ** (-jnp.arange(0, HEAD_DIM, 2, dtype=F32) / HEAD_DIM)
    ang = pos[:, None] * inv[None, :]
    cos, sin = jnp.cos(ang), jnp.sin(ang)
    cos_t = jnp.concatenate([cos, cos] * HEADS_PER_GROUP, axis=-1)
    sin_t = jnp.concatenate([-sin, sin] * HEADS_PER_GROUP, axis=-1)
    return cos_t, sin_t


def kernel(x, c, a_ada_w, a_ada_b, a_norm_g, a_mix_mu, a_w_in, a_w0, a_w1, a_w2, a_a0, a_a1, a_a2,
           a_k_k, a_k_a, a_r_k, a_ln_g, a_ln_b, a_w_out, kv_norm_g, w_kv, k_norm_g,
           b_ada_w, b_ada_b, b_norm_g, b_w_in, b_q_norm_g, b_w_out):
    B, S, D = x.shape
    assert a_ada_w.shape[0] == 1 and b_ada_w.shape[0] == 1
    assert D % LANES == 0 and S % (BAND_BLOCK * DIL_GROUPS[-1][1]) == 0 and S % SEQ_TILE == 0
    assert all(win // dil == BAND_BLOCK for win, dil in DIL_GROUPS)
    assert b_w_in.shape[-1] == (len(DIL_GROUPS) + 1) * D
    row = lambda t: t.reshape(1, -1)
    per_head = lambda t: jnp.tile(t.reshape(1, HEAD_DIM), (1, D // HEAD_DIM))

    mod_a, mod_b = _adaln(c, a_ada_w, a_ada_b, b_ada_w, b_ada_b)
    mod_a = mod_a.reshape(B, 1, 3 * D)
    mod_b = mod_b.reshape(B, 1, 3 * D)
    cos_t, sin_t = _rope_tables(S)

    r, lw, k, v, an, bn, sg_a = _a_pre(
        x, mod_a, a_norm_g, a_mix_mu[0], a_w_in[0].astype(BF16), a_w0, a_w1[0].astype(BF16),
        a_w2[0].astype(BF16), a_a0, a_a1[0].astype(BF16), a_a2[0].astype(BF16), a_k_k, a_k_a)
    y = _wkv(r, lw, k, v, an, bn)
    xr, k_sh, v_sh = _a_post(
        y, r, k, v, sg_a, x, mod_a, a_ln_g, a_ln_b, row(a_r_k[0]), a_w_out[0].astype(BF16),
        row(kv_norm_g), w_kv.astype(BF16), per_head(k_norm_g), cos_t, sin_t)

    q, sg_b = _b_pre(xr, mod_b, b_norm_g, b_w_in[0].astype(BF16), per_head(b_q_norm_g[0]), cos_t, sin_t)
    att = _attn(q, k_sh, v_sh, sg_b)
    return _b_post(att, xr, mod_b, b_w_out[0].astype(BF16))
```

```python
import functools

import jax
import jax.numpy as jnp
from jax import lax
from jax.experimental import pallas as pl
from jax.experimental.pallas import tpu as pltpu

F32 = jnp.float32
BF16 = jnp.bfloat16

HEAD_DIM = 64
LANES = 128
HEADS_PER_GROUP = LANES // HEAD_DIM
DIL_GROUPS = ((128, 1), (512, 4), (2048, 16))
BAND_BLOCK = 128
ROPE_THETA = 10000.0
NORM_EPS = 1e-6
GN_EPS = 64e-5
NEG_INF = -1e30
DECAY_SCALE = 0.6065306597126334
CHUNK = 64
WKV_PHASE_GAP = 4
WKV_SCAN_GAP = 2
ATTN_PHASE_GAP = 2
ATTN_SLOTS = 8
SEQ_TILE = 512
Q_SCALE = HEAD_DIM ** -0.5 * 1.4426950408889634
VMEM_LIMIT = 56 * 1024 * 1024

_NT = (((1,), (1,)), ((), ()))
_TN = (((0,), (0,)), ((), ()))


def _dot(a, b):
    return jnp.dot(a, b, preferred_element_type=F32)


def _mm(a, b):
    return _dot(a.astype(BF16), b.astype(BF16))


def _split_dot_lhs(e, x):
    hi = x.astype(BF16)
    lo = (x - hi.astype(F32)).astype(BF16)
    return _dot(e, hi) + _dot(e, lo)


def _mm_3pass(a, b):
    a_hi = a.astype(BF16)
    a_lo = (a - a_hi.astype(F32)).astype(BF16)
    b_hi = b.astype(BF16)
    b_lo = (b - b_hi.astype(F32)).astype(BF16)
    return _dot(a_hi, b_hi) + (_dot(a_hi, b_lo) + _dot(a_lo, b_hi))


def _head_sum(x, e):
    parts = [_dot(x[:, g * LANES:(g + 1) * LANES].astype(BF16), e) for g in range(x.shape[1] // LANES)]
    return parts[0] if len(parts) == 1 else jnp.concatenate(parts, axis=1)


def _rot_half(x):
    lane = lax.broadcasted_iota(jnp.int32, (1, LANES), 1)
    first = (lane % HEAD_DIM) < (HEAD_DIM // 2)
    parts = []
    for g in range(x.shape[1] // LANES):
        xg = x[:, g * LANES:(g + 1) * LANES]
        up = pltpu.roll(xg, LANES - HEAD_DIM // 2, axis=1)
        dn = pltpu.roll(xg, HEAD_DIM // 2, axis=1)
        parts.append(jnp.where(first, up, dn))
    return parts[0] if len(parts) == 1 else jnp.concatenate(parts, axis=1)


def _tile_lanes(t, n):
    return t if n == 1 else jnp.concatenate([t] * n, axis=1)


def _rms_rows(x):
    return x * lax.rsqrt(jnp.mean(x * x, axis=-1, keepdims=True) + NORM_EPS)


def _sigmoid(x):
    return 1.0 / (1.0 + jnp.exp(-x))


def _head_ones():
    r = lax.broadcasted_iota(jnp.int32, (LANES, LANES), 0) // HEAD_DIM
    c = lax.broadcasted_iota(jnp.int32, (LANES, LANES), 1) // HEAD_DIM
    return (r == c).astype(BF16)


def _adaln_kernel(c_ref, wa_ref, ba_ref, wb_ref, bb_ref, oa_ref, ob_ref):
    c = c_ref[...]
    sc = c * _sigmoid(c)
    oa_ref[...] = _mm_3pass(sc, wa_ref[...]) + ba_ref[...]
    ob_ref[...] = _mm_3pass(sc, wb_ref[...]) + bb_ref[...]


def _adaln(c, wa, ba, wb, bb):
    B, D = c.shape
    n3 = wa.shape[-1]
    tn = 512
    wspec = pl.BlockSpec((None, D, tn), lambda j: (0, 0, j))
    bspec = pl.BlockSpec((1, tn), lambda j: (0, j))
    ospec = pl.BlockSpec((B, tn), lambda j: (0, j))
    return pl.pallas_call(
        _adaln_kernel,
        grid=(n3 // tn,),
        in_specs=[pl.BlockSpec((B, D), lambda j: (0, 0)), wspec, bspec, wspec, bspec],
        out_specs=[ospec, ospec],
        out_shape=[jax.ShapeDtypeStruct((B, n3), F32)] * 2,
        name="adaln",
    )(c, wa, ba, wb, bb)


def _a_pre_kernel(x_ref, xp_ref, mod_ref, ng_ref, mu_ref, win_ref, w0_ref, w1_ref, w2_ref,
                  a0_ref, a1_ref, a2_ref, kk_ref, ka_ref,
                  r_out, lw_out, k_out, v_out, an_out, bn_out, sg_out):
    D = x_ref.shape[-1]
    i = pl.program_id(1)
    mod = mod_ref[...]
    shift, scale = mod[:, :D], mod[:, D:2 * D]
    gain = ng_ref[...] * (1.0 + scale)

    h = _rms_rows(x_ref[...]) * gain + shift
    hp = _rms_rows(xp_ref[...]) * gain + shift
    hp = jnp.where(i == 0, 0.0, hp[7:8, :])
    row = lax.broadcasted_iota(jnp.int32, h.shape, 0)
    hs = jnp.where(row == 0, hp, pltpu.roll(h, 1, axis=0))
    xx = hs - h
    mu = mu_ref[...]
    lerp = lambda p: (h + xx * mu[p:p + 1, :]).astype(BF16)
    wl = _mm(jnp.tanh(_mm(lerp(4), w1_ref[...])), w2_ref[...])
    al = _mm(_mm(lerp(5), a1_ref[...]), a2_ref[...])
    r = _dot(lerp(0), win_ref[:, 0 * D:1 * D])
    k = _dot(lerp(1), win_ref[:, 1 * D:2 * D])
    v = _dot(lerp(2), win_ref[:, 2 * D:3 * D])
    g = _dot(lerp(3), win_ref[:, 3 * D:4 * D])

    a = _sigmoid(a0_ref[...] + al)

    kk = k * kk_ref[...]
    ss = _head_sum(kk * kk, _head_ones())
    kk = kk * lax.rsqrt(jnp.maximum(ss, 1e-24))

    r_out[...] = r.astype(r_out.dtype)
    lw_out[...] = (-DECAY_SCALE) * _sigmoid(w0_ref[...] + wl)
    k_out[...] = (k * (1.0 + (a - 1.0) * ka_ref[...])).astype(k_out.dtype)
    v_out[...] = v.astype(v_out.dtype)
    an_out[...] = (-kk).astype(an_out.dtype)
    bn_out[...] = (kk * a).astype(bn_out.dtype)
    sg_out[...] = (g * _sigmoid(g)).astype(sg_out.dtype)


def _a_pre(x, mod3, ng, mu, w_in, w0, w1, w2, a0, a1, a2, k_k, k_a):
    B, S, D = x.shape
    ts = SEQ_TILE
    tile = pl.BlockSpec((None, ts, D), lambda b, i: (b, i, 0))
    prev = pl.BlockSpec((None, 8, D), lambda b, i: (b, jnp.maximum(i * (ts // 8) - 1, 0), 0))
    vec = pl.BlockSpec((1, D), lambda b, i: (0, 0))

    def full(a):
        return pl.BlockSpec(a.shape, lambda b, i: (0,) * a.ndim, pipeline_mode=pl.Buffered(1))

    act = lambda dt: jax.ShapeDtypeStruct((B, S, D), dt)
    return pl.pallas_call(
        _a_pre_kernel,
        grid=(B, S // ts),
        in_specs=[tile, prev, pl.BlockSpec((None, 1, 3 * D), lambda b, i: (b, 0, 0)), vec, full(mu),
                  full(w_in), vec, full(w1), full(w2), vec, full(a1), full(a2), vec, vec],
        out_specs=[tile] * 7,
        out_shape=[act(BF16), act(F32), act(BF16), act(BF16), act(BF16), act(BF16), act(BF16)],
        compiler_params=pltpu.CompilerParams(vmem_limit_bytes=VMEM_LIMIT),
        name="a_pre",
    )(x, x, mod3, ng, mu, w_in, w0, w1, w2, a0, a1, a2, k_k, k_a)


def _wkv_kernel(r_ref, lw_ref, k_ref, v_ref, a_ref, b_ref, y_ref,
                lp_scr, t_scr, aak_scr, ark_scr, arb_scr, ats_scr, vs_scr, rt_scr, kend_scr, bend_scr,
                wend_scr, rh_scr, y0_scr, p_scr, q_scr, st_scr):
    C = CHUNK
    n_chunks = r_ref.shape[0] // C
    assert C == HEAD_DIM
    step = pl.program_id(0)
    cur = step % 2
    prev = 1 - cur

    lane = lax.broadcasted_iota(jnp.int32, (1, LANES), 1)
    head0 = lane < HEAD_DIM
    ti = lax.broadcasted_iota(jnp.int32, (C, LANES), 0)
    ii = lax.broadcasted_iota(jnp.int32, (C, LANES), 1) % C
    strict = ii < ti
    incl = ii <= ti
    eye_pair = (ii == ti).astype(F32)
    ri = lax.broadcasted_iota(jnp.int32, (LANES, LANES), 0)
    ci = lax.broadcasted_iota(jnp.int32, (LANES, LANES), 1)
    same = (ri // HEAD_DIM) == (ci // HEAD_DIM)
    tri = (lax.broadcasted_iota(jnp.int32, (C, C), 1)
           <= lax.broadcasted_iota(jnp.int32, (C, C), 0)).astype(BF16)

    def stack(x):
        zero = jnp.zeros_like(x)
        return jnp.concatenate([jnp.where(head0, x, zero), jnp.where(head0, zero, x)], axis=0)

    def chunk_rows(c):
        return slice(c * C, (c + 1) * C)

    def gram_terms(c):
        rows = chunk_rows(c)
        cum = _split_dot_lhs(tri, lw_ref[rows, :])
        yield
        lw = lw_ref[rows, :]
        r, k, v, a, b = (ref[rows, :].astype(F32) for ref in (r_ref, k_ref, v_ref, a_ref, b_ref))
        e_cum = jnp.exp(cum)
        e_inv = jnp.exp(-cum)
        e_end = jnp.exp(cum[C - 1:C, :] - cum)
        rt = r * e_cum
        at = (a * jnp.exp(cum - lw)).astype(BF16)
        kt = (k * e_inv).astype(BF16)
        bt = (b * e_inv).astype(BF16)
        kb = jnp.concatenate([stack(kt), stack(bt)], axis=0)
        ga = lax.dot_general(at, kb, _NT, preferred_element_type=F32)
        gr = lax.dot_general(rt.astype(BF16), kb, _NT, preferred_element_type=F32)
        ats_scr[c] = stack(at)
        vs_scr[c] = stack(v.astype(BF16))
        rt_scr[rows, :] = rt
        kend_scr[rows, :] = (k * e_end).astype(BF16)
        bend_scr[rows, :] = (b * e_end).astype(BF16)
        wend_scr[c] = jnp.broadcast_to(e_cum[C - 1:C, :], (8, LANES))
        yield
        a_ab = jnp.where(strict, ga[:, LANES:], 0.0)
        lp_scr[c] = a_ab.astype(BF16)
        t_scr[c] = eye_pair + a_ab
        aak_scr[c] = jnp.where(strict, ga[:, :LANES], 0.0).astype(BF16)
        ark_scr[c] = jnp.where(incl, gr[:, :LANES], 0.0).astype(BF16)
        arb_scr[c] = jnp.where(incl, gr[:, LANES:], 0.0).astype(BF16)

    def square_only(c):
        lp = lp_scr[c]
        lp2 = _dot(lp, stack(lp))
        yield
        lp_scr[c] = lp2.astype(BF16)

    def fold_and_square(c):
        lp = lp_scr[c]
        lp_bd = stack(lp)
        t = t_scr[c]
        tl = _dot(t.astype(BF16), lp_bd)
        lp2 = _dot(lp, lp_bd)
        yield
        t_scr[c] = t + tl
        lp_scr[c] = lp2.astype(BF16)

    def fold_only(c):
        t = t_scr[c]
        tl = _dot(t.astype(BF16), stack(lp_scr[c]))
        yield
        t_scr[c] = t + tl

    def chunk_maps(c):
        rows = chunk_rows(c)
        vs = vs_scr[c]
        x = _dot(aak_scr[c], vs)
        y0a = _dot(ark_scr[c], vs)
        yield
        au = _dot(t_scr[c].astype(BF16),
                  jnp.concatenate([ats_scr[c], stack(x.astype(BF16))], axis=1))
        yield
        ah = au[:, :LANES].astype(BF16)
        u0 = au[:, LANES:].astype(BF16)
        ry = _dot(arb_scr[c], jnp.concatenate([stack(ah), stack(u0)], axis=1))
        k_end = kend_scr[rows, :]
        b_end = bend_scr[rows, :]
        p = lax.dot_general(b_end, ah, _TN, preferred_element_type=F32)
        q = lax.dot_general(jnp.concatenate([k_end, b_end], axis=0),
                            jnp.concatenate([v_ref[rows, :], u0], axis=0), _TN,
                            preferred_element_type=F32)
        yield
        rh_scr[cur, rows, :] = rt_scr[rows, :] + ry[:, :LANES]
        y0_scr[cur, rows, :] = y0a + ry[:, LANES:]
        p = jnp.where(same, p, 0.0)
        p_pair = p[:C, :] + p[C:, :] + eye_pair * wend_scr[c][0:1, :]
        p_hi = p_pair.astype(BF16)
        p_scr[cur, c] = jnp.concatenate([p_hi, (p_pair - p_hi.astype(F32)).astype(BF16)], axis=1)
        q = jnp.where(same, q, 0.0)
        q_scr[cur, c] = q[:C, :] + q[C:, :]

    def scan_chunks():
        for c in range(n_chunks):
            rows = chunk_rows(c)
            st = st_scr[...]
            st_hi = st.astype(BF16)
            st_lo = (st - st_hi.astype(F32)).astype(BF16)
            hi_bd = stack(st_hi)
            pc = p_scr[prev, c]
            p_hi = pc[:, :LANES]
            st_new = _dot(p_hi, hi_bd) + (_dot(p_hi, stack(st_lo)) + _dot(pc[:, LANES:], hi_bd))
            ys = _dot(rh_scr[prev, rows, :].astype(BF16), hi_bd)
            yield
            y_ref[rows, :] = (y0_scr[prev, rows, :] + ys).astype(y_ref.dtype)
            st_scr[...] = st_new + q_scr[prev, c]

    stages = [gram_terms, square_only]
    n = 4
    while n < C:
        stages.append(fold_and_square)
        n *= 2
    stages += [fold_only, chunk_maps]

    @pl.when(step == 0)
    def _():
        rh_scr[1] = jnp.zeros(rh_scr.shape[1:], F32)
        y0_scr[1] = jnp.zeros(y0_scr.shape[1:], F32)
        p_scr[1] = jnp.zeros(p_scr.shape[1:], BF16)
        q_scr[1] = jnp.zeros(q_scr.shape[1:], F32)

    def chunk_pipeline(c):
        for stage in stages:
            yield from stage(c)

    st_scr[...] = jnp.zeros((C, LANES), F32)
    gens =[chunk_pipeline(c) for c in range(n_chunks)]
    scan = scan_chunks()
    live = set(range(n_chunks))
    t = 0
    scan_live = True
    while live or scan_live:
        for i in sorted(live):
            if t >= i and (t - i) % WKV_PHASE_GAP == 0:
                try:
                    next(gens[i])
                except StopIteration:
                    live.discard(i)
        if scan_live and t % WKV_SCAN_GAP == 0:
            try:
                next(scan)
            except StopIteration:
                scan_live = False
        t += 1


def _wkv(r, lw, k, v, an, bn):
    B, S, D = r.shape
    n_pairs = D // LANES
    n_steps = B * n_pairs
    n_chunks = S // CHUNK
    in_blk = pl.BlockSpec((None, S, LANES), lambda s: (jnp.minimum(s, n_steps - 1) // n_pairs, 0,
                                                       jnp.minimum(s, n_steps - 1) % n_pairs))
    out_blk = pl.BlockSpec((None, S, LANES), lambda s: (jnp.maximum(s - 1, 0) // n_pairs, 0,
                                                        jnp.maximum(s - 1, 0) % n_pairs))
    mat = lambda dt: pltpu.VMEM((n_chunks, LANES, LANES), dt)
    pair = lambda dt: pltpu.VMEM((n_chunks, CHUNK, LANES), dt)
    seq = lambda dt: pltpu.VMEM((S, LANES), dt)
    return pl.pallas_call(
        _wkv_kernel,
        grid=(n_steps + 1,),
        in_specs=[in_blk] * 6,
        out_specs=out_blk,
        out_shape=jax.ShapeDtypeStruct((B, S, D), BF16),
        scratch_shapes=[pair(BF16), pair(F32), pair(BF16), pair(BF16), pair(BF16), mat(BF16), mat(BF16),
                        seq(F32), seq(BF16), seq(BF16), pltpu.VMEM((n_chunks, 8, LANES), F32),
                        pltpu.VMEM((2, S, LANES), F32), pltpu.VMEM((2, S, LANES), F32),
                        pltpu.VMEM((2, n_chunks, CHUNK, 2 * LANES), BF16),
                        pltpu.VMEM((2, n_chunks, CHUNK, LANES), F32),
                        pltpu.VMEM((CHUNK, LANES), F32)],
        compiler_params=pltpu.CompilerParams(vmem_limit_bytes=VMEM_LIMIT,
                                             dimension_semantics=("arbitrary",)),
        name="wkv",
    )(r, lw, k, v, an, bn)


def _a_post_kernel(y_ref, r_ref, k_ref, v_ref, sg_ref, x_ref, mod_ref, lng_ref, lnb_ref, rk_ref,
                   wout_ref, kvg_ref, wkv_ref, kng_ref, cos_ref, sin_ref,
                   xr_out, ksh_out, vsh_out):
    D = x_ref.shape[-1]
    e = _head_ones()
    n_lane_groups = D // LANES
    gate = mod_ref[...][:, 2 * D:]
    half = x_ref.shape[0] // 2
    halves = [slice(i * half, (i + 1) * half) for i in range(2)]
    f32 = lambda ref, rows: ref[rows, :].astype(F32)
    pre = []
    for rows in halves:
        y = f32(y_ref, rows)
        mean = _head_sum(y, e) * (1.0 / HEAD_DIM)
        d = y - mean
        var = _head_sum(d * d, e) * (1.0 / HEAD_DIM)
        yn = d * lax.rsqrt(var + GN_EPS) * lng_ref[...] + lnb_ref[...]
        bonus = _head_sum(f32(r_ref, rows) * f32(k_ref, rows) * rk_ref[...], e) * f32(v_ref, rows)
        pre.append(((yn + bonus) * f32(sg_ref, rows)).astype(BF16))
    xrs = []
    for rows, z in zip(halves, pre):
        xr = x_ref[rows, :] + gate * _dot(z, wout_ref[...])
        xr_out[rows, :] = xr
        xrs.append(xr)
    kvs = [_mm(_rms_rows(xr) * kvg_ref[...], wkv_ref[...]) for xr in xrs]
    for rows, kv in zip(halves, kvs):
        ks = kv[:, :D]
        ks = ks * lax.rsqrt(_head_sum(ks * ks, e) * (1.0 / HEAD_DIM) + NORM_EPS) * kng_ref[...]
        cos = _tile_lanes(cos_ref[rows, :], n_lane_groups)
        sin = _tile_lanes(sin_ref[rows, :], n_lane_groups)
        ksh_out[rows, :] = (ks * cos + _rot_half(ks) * sin).astype(ksh_out.dtype)
        vsh_out[rows, :] = kv[:, D:].astype(vsh_out.dtype)


def _a_post(y, r, k, v, sg, x, mod3, ln_g, ln_b, r_k, w_out, kv_g, w_kv, kn_g, cos_t, sin_t):
    B, S, D = x.shape
    ts = SEQ_TILE
    tile = pl.BlockSpec((None, ts, D), lambda b, i: (b, i, 0))
    vec = pl.BlockSpec((1, D), lambda b, i: (0, 0))
    rope = pl.BlockSpec((ts, LANES), lambda b, i: (i, 0))

    def full(a):
        return pl.BlockSpec(a.shape, lambda b, i: (0,) * a.ndim)

    act = lambda dt: jax.ShapeDtypeStruct((B, S, D), dt)
    return pl.pallas_call(
        _a_post_kernel,
        grid=(B, S // ts),
        in_specs=[tile] * 6 + [pl.BlockSpec((None, 1, 3 * D), lambda b, i: (b, 0, 0)), vec, vec, vec,
                               full(w_out), vec, full(w_kv), vec, rope, rope],
        out_specs=[tile] * 3,
        out_shape=[act(F32), act(BF16), act(BF16)],
        compiler_params=pltpu.CompilerParams(vmem_limit_bytes=VMEM_LIMIT),
        name="a_post",
    )(y, r, k, v, sg, x, mod3, ln_g, ln_b, r_k, w_out, kv_g, w_kv, kn_g, cos_t, sin_t)


def _b_pre_kernel(x_ref, mod_ref, ng_ref, win_ref, qg_ref, cos_ref, sin_ref, q_out, sg_out):
    D = x_ref.shape[-1]
    nq = q_out.shape[-1]
    e = _head_ones()
    mod = mod_ref[...]
    shift, scale = mod[:, :D], mod[:, D:2 * D]
    h = (_rms_rows(x_ref[...]) * (ng_ref[...] * (1.0 + scale)) + shift).astype(BF16)
    cos = _tile_lanes(cos_ref[...], D // LANES)
    sin = _tile_lanes(sin_ref[...], D // LANES)
    for g in range(nq // D):
        q = _dot(h, win_ref[:, g * D:(g + 1) * D])
        q = q * lax.rsqrt(_head_sum(q * q, e) * (1.0 / HEAD_DIM) + NORM_EPS) * qg_ref[...]
        q_out[:, g * D:(g + 1) * D] = ((q * cos + _rot_half(q) * sin) * Q_SCALE).astype(q_out.dtype)
    gate = _dot(h, win_ref[:, nq:])
    sg_out[...] = (gate * _sigmoid(gate)).astype(sg_out.dtype)


def _b_pre(xr, mod3, ng, w_in, qn_g, cos_t, sin_t):
    B, S, D = xr.shape
    nq = w_in.shape[1] - D
    ts = SEQ_TILE
    tile = pl.BlockSpec((None, ts, D), lambda b, i: (b, i, 0))
    vec = pl.BlockSpec((1, D), lambda b, i: (0, 0))
    rope = pl.BlockSpec((ts, LANES), lambda b, i: (i, 0))
    return pl.pallas_call(
        _b_pre_kernel,
        grid=(B, S // ts),
        in_specs=[tile, pl.BlockSpec((None, 1, 3 * D), lambda b, i: (b, 0, 0)), vec,
                  pl.BlockSpec(w_in.shape, lambda b, i: (0, 0)), vec, rope, rope],
        out_specs=[pl.BlockSpec((None, ts, nq), lambda b, i: (b, i, 0)), tile],
        out_shape=[jax.ShapeDtypeStruct((B, S, nq), BF16), jax.ShapeDtypeStruct((B, S, D), BF16)],
        compiler_params=pltpu.CompilerParams(vmem_limit_bytes=VMEM_LIMIT),
        name="b_pre",
    )(xr, mod3, ng, w_in, qn_g, cos_t, sin_t)


def _attn_kernel(q0_ref, q1_ref, q2_ref, k_ref, v_ref, sg_ref, o_ref,
                 tmp_scr, tmp2_scr, qr_scr, kr_scr, vr_scr, acc_scr, m_scr, l_scr, s_scr, p_scr):
    S = k_ref.shape[0]
    blk = BAND_BLOCK
    n_groups = len(DIL_GROUPS)
    lane = lax.broadcasted_iota(jnp.int32, (1, LANES), 1)
    head0 = lane < HEAD_DIM

    def head_ones(nk):
        hsel = (lax.broadcasted_iota(jnp.int32, (HEADS_PER_GROUP * nk, LANES), 1) // HEAD_DIM
                == lax.broadcasted_iota(jnp.int32, (HEADS_PER_GROUP * nk, LANES), 0) // nk)
        return hsel.astype(F32).astype(BF16)

    def upcast(src_ref):
        rows_per = 8 * blk

        def tile(t, carry):
            rows = pl.ds(pl.multiple_of(t * rows_per, rows_per), rows_per)
            tmp_scr[rows, :] = src_ref[rows, :].astype(F32)
            return carry
        lax.fori_loop(0, S // rows_per, tile, 0)

    (_, d1), (_, d2) = DIL_GROUPS[1], DIL_GROUPS[2]
    assert d2 == d1 * d1 and DIL_GROUPS[0][1] == 1
    seg1, seg2 = S // d1, S // d2
    for src_ref, dst_scr, slots in ((q1_ref, qr_scr, (0,)), (q2_ref, qr_scr, (1,)),
                                    (k_ref, kr_scr, (0, 1)), (v_ref, vr_scr, (0, 1))):
        upcast(src_ref)
        for rho in range(d1):
            x = tmp_scr[pl.ds(rho, seg1, stride=d1), :]
            if 0 in slots:
                dst_scr[0, rho * seg1:(rho + 1) * seg1, :] = x.astype(BF16)
            if 1 in slots:
                tmp2_scr[rho * seg1:(rho + 1) * seg1, :] = x
        if 1 in slots:
            for rho in range(d1):
                for sub in range(d1):
                    r16 = rho + d1 * sub
                    dst_scr[1, r16 * seg2:(r16 + 1) * seg2, :] = (
                        tmp2_scr[pl.ds(rho * seg1 + sub, seg2, stride=d1), :].astype(BF16))

    def geometry(gi, rho, n):
        dil = DIL_GROUPS[gi][1]
        j = rho * (S // dil // blk) + n
        lo = (j - 1) * blk if n > 0 else j * blk
        nk = (j + 1) * blk - lo
        if gi == 0:
            srcs = (q0_ref, k_ref, v_ref)
        else:
            srcs = (qr_scr.at[gi - 1], kr_scr.at[gi - 1], vr_scr.at[gi - 1])
        return j, lo, nk, srcs

    def out_rows(gi, rho, n, start, size):
        dil = DIL_GROUPS[gi][1]
        first = rho + dil * (blk * n + start)
        if dil == 1:
            return gi, slice(first, first + size)
        if dil == d1:
            return gi, pl.ds(first, size, stride=dil)
        return n_groups, pl.ds((first % d1) * seg1 + first // d1, size, stride=dil // d1)

    def per_head(x, other):
        return jnp.concatenate([jnp.where(head0, x, other), jnp.where(head0, other, x)], axis=0)

    def scores(b, slot):
        j, lo, nk, (q_src, k_src, _) = geometry(*b)
        kc = k_src[lo:lo + nk, :]
        s_scr[slot, :, :2 * nk] = lax.dot_general(q_src[j * blk:(j + 1) * blk, :],
                                                  per_head(kc, jnp.zeros_like(kc)), _NT,
                                                  preferred_element_type=F32)

    def softmax(b, slot):
        gi = b[0]
        win, dil = DIL_GROUPS[gi]
        nk = geometry(*b)[2]
        rt = blk // 2
        qi = lax.broadcasted_iota(jnp.int32, (rt, nk), 0)
        kj = lax.broadcasted_iota(jnp.int32, (rt, nk), 1)
        for t in range(blk // rt):
            diff = (nk - blk) + (qi + t * rt) - kj
            valid = (diff >= 0) & (diff <= win // dil)
            ms = []
            for h in range(HEADS_PER_GROUP):
                s = jnp.where(valid, s_scr[slot, t * rt:(t + 1) * rt, h * nk:(h + 1) * nk], NEG_INF)
                m = jnp.max(s, axis=-1, keepdims=True)
                p_scr[slot, t * rt:(t + 1) * rt, h * nk:(h + 1) * nk] = jnp.exp2(s - m).astype(BF16)
                ms.append(m)
            slot_rows = out_rows(*b, t * rt, rt)
            m_scr[slot_rows[0], slot_rows[1], :] = jnp.where(head0, ms[0], ms[1])

    def weighted_values(b, slot):
        gi = b[0]
        _, lo, nk, (_, _, v_src) = geometry(*b)
        vc = v_src[lo:lo + nk, :]
        zero = jnp.zeros_like(vc)
        v2 = jnp.concatenate([per_head(vc, zero), head_ones(nk)], axis=1)
        pvl = _dot(p_scr[slot, :, :2 * nk], v2)
        slot, rows = out_rows(*b, 0, blk)
        acc_scr[slot, rows, :] = pvl[:, :LANES]
        l_scr[slot, rows, :] = pvl[:, LANES:]

    blocks = [(gi, rho, n) for gi, (_, dil) in enumerate(DIL_GROUPS)
              for rho in range(dil) for n in range(S // dil // blk)]
    gap = ATTN_PHASE_GAP
    n_slots = s_scr.shape[0]
    assert n_slots > 2 * gap
    for t in range(len(blocks) + 2 * gap):
        if 0 <= t - 2 * gap < len(blocks):
            weighted_values(blocks[t - 2 * gap], (t - 2 * gap) % n_slots)
        if t < len(blocks):
            scores(blocks[t], t % n_slots)
        if 0 <= t - gap < len(blocks):
            softmax(blocks[t - gap], (t - gap) % n_slots)

    for scr in (acc_scr, m_scr, l_scr):
        for rho in range(d1):
            scr[n_groups - 1, pl.ds(rho, seg1, stride=d1), :] = scr[n_groups, rho * seg1:(rho + 1) * seg1, :]

    merge_rows = 4 * blk

    def merge(t, carry):
        rows = pl.ds(pl.multiple_of(t * merge_rows, merge_rows), merge_rows)
        ms = [m_scr[g, rows, :] for g in range(n_groups)]
        m_all = functools.reduce(jnp.maximum, ms)
        ws = [jnp.exp2(m - m_all) for m in ms]
        num = functools.reduce(jnp.add, [w * acc_scr[g, rows, :] for g, w in enumerate(ws)])
        den = functools.reduce(jnp.add, [w * l_scr[g, rows, :] for g, w in enumerate(ws)])
        o_ref[rows, :] = (num / den * sg_ref[rows, :].astype(F32)).astype(o_ref.dtype)
        return carry

    lax.fori_loop(0, S // merge_rows, merge, 0)


def _attn(q, k_sh, v_sh, sg):
    B, S, D = k_sh.shape
    n_groups = len(DIL_GROUPS)
    n_pairs = D // LANES
    qspec = lambda g: pl.BlockSpec((None, S, LANES), lambda b, p: (b, 0, g * n_pairs + p))
    blk = pl.BlockSpec((None, S, LANES), lambda b, p: (b, 0, p))
    res = lambda: pltpu.VMEM((n_groups - 1, S, LANES), BF16)
    nat = lambda: pltpu.VMEM((n_groups + 1, S, LANES), F32)
    return pl.pallas_call(
        _attn_kernel,
        grid=(B, n_pairs),
        in_specs=[qspec(0), qspec(1), qspec(2), blk, blk, blk],
        out_specs=blk,
        out_shape=jax.ShapeDtypeStruct((B, S, D), BF16),
        scratch_shapes=[pltpu.VMEM((S, LANES), F32), pltpu.VMEM((S, LANES), F32), res(), res(), res(),
                        nat(), nat(), nat(),
                        pltpu.VMEM((ATTN_SLOTS, BAND_BLOCK, 2 * HEADS_PER_GROUP * BAND_BLOCK), F32),
                        pltpu.VMEM((ATTN_SLOTS, BAND_BLOCK, 2 * HEADS_PER_GROUP * BAND_BLOCK), BF16)],
        compiler_params=pltpu.CompilerParams(vmem_limit_bytes=VMEM_LIMIT),
        name="attn",
    )(q, q, q, k_sh, v_sh, sg)


def _b_post_kernel(att_ref, x_ref, mod_ref, wout_ref, out_ref):
    D = x_ref.shape[-1]
    gate = mod_ref[...][:, 2 * D:]
    out_ref[...] = x_ref[...] + gate * _dot(att_ref[...], wout_ref[...])


def _b_post(att, xr, mod3, w_out):
    B, S, D = xr.shape
    ts = SEQ_TILE
    tile = pl.BlockSpec((None, ts, D), lambda b, i: (b, i, 0))
    return pl.pallas_call(
        _b_post_kernel,
        grid=(B, S // ts),
        in_specs=[tile, tile, pl.BlockSpec((None, 1, 3 * D), lambda b, i: (b, 0, 0)),
                  pl.BlockSpec(w_out.shape, lambda b, i: (0, 0))],
        out_specs=tile,
        out_shape=jax.ShapeDtypeStruct((B, S, D), F32),
        compiler_params=pltpu.CompilerParams(vmem_limit_bytes=VMEM_LIMIT),
        name="b_post",
    )(att, xr, mod3, w_out)


def _rope_tables(seq):
    pos = jnp.arange(seq, dtype=F32)
    inv = ROPE_THETA ** (-jnp.arange(0, HEAD_DIM, 2, dtype=F32) / HEAD_DIM)
    ang = pos[:, None] * inv[None, :]
    cos, sin = jnp.cos(ang), jnp.sin(ang)
    cos_t = jnp.concatenate([cos, cos] * HEADS_PER_GROUP, axis=-1)
    sin_t = jnp.concatenate([-sin, sin] * HEADS_PER_GROUP, axis=-1)
    return cos_t, sin_t


def kernel(x, c, a_ada_w, a_ada_b, a_norm_g, a_mix_mu, a_w_in, a_w0, a_w1, a_w2, a_a0, a_a1, a_a2,
           a_k_k, a_k_a, a_r_k, a_ln_g, a_ln_b, a_w_out, kv_norm_g, w_kv, k_norm_g,
           b_ada_w, b_ada_b, b_norm_g, b_w_in, b_q_norm_g, b_w_out):
    B, S, D = x.shape
    assert a_ada_w.shape[0] == 1 and b_ada_w.shape[0] == 1
    assert D % LANES == 0 and S % (BAND_BLOCK * DIL_GROUPS[-1][1]) == 0 and S % SEQ_TILE == 0
    assert all(win // dil == BAND_BLOCK for win, dil in DIL_GROUPS)
    assert b_w_in.shape[-1] == (len(DIL_GROUPS) + 1) * D
    row = lambda t: t.reshape(1, -1)
    per_head = lambda t: jnp.tile(t.reshape(1, HEAD_DIM), (1, D // HEAD_DIM))

    mod_a, mod_b = _adaln(c, a_ada_w, a_ada_b, b_ada_w, b_ada_b)
    mod_a = mod_a.reshape(B, 1, 3 * D)
    mod_b = mod_b.reshape(B, 1, 3 * D)
    cos_t, sin_t = _rope_tables(S)

    r, lw, k, v, an, bn, sg_a = _a_pre(
        x, mod_a, a_norm_g, a_mix_mu[0], a_w_in[0].astype(BF16), a_w0, a_w1[0].astype(BF16),
        a_w2[0].astype(BF16), a_a0, a_a1[0].astype(BF16), a_a2[0].astype(BF16), a_k_k, a_k_a)
    y = _wkv(r, lw, k, v, an, bn)
    xr, k_sh, v_sh = _a_post(
        y, r, k, v, sg_a, x, mod_a, a_ln_g, a_ln_b, row(a_r_k[0]), a_w_out[0].astype(BF16),
        row(kv_norm_g), w_kv.astype(BF16), per_head(k_norm_g), cos_t, sin_t)

    q, sg_b = _b_pre(xr, mod_b, b_norm_g, b_w_in[0].astype(BF16), per_head(b_q_norm_g[0]), cos_t, sin_t)
    att = _attn(q, k_sh, v_sh, sg_b)
    return _b_post(att, xr, mod_b, b_w_out[0].astype(BF16))
```
